```python
import math
import jax, jax.numpy as jnp
from jax import lax
import numpy as np

D_MODEL = 1024
BATCH = 8
SEQ = 2048
DEPTH = 2

CHUNK = 64
Q_BLOCK = 128
ROPE_THETA = 500000.0
N_MIXERS = 2
N_A_LAYERS = (DEPTH + 1) // 2
N_B_LAYERS = DEPTH // 2

A_HEADS = 8
A_HEAD_DIM = 64
A_ROT = A_HEAD_DIM // 4
A_QK = A_HEADS * 2 * A_HEAD_DIM
A_V = A_HEADS * 2 * A_HEAD_DIM
A_WIDTH = A_V
A_IN = 2 * A_QK + A_V + A_WIDTH

B_HEADS = 16
B_NOPE = 64
B_ROPE = 32
B_VDIM = 64
B_Q_RANK = 512
B_KV_RANK = 256
B_WIDTH = B_HEADS * B_VDIM
B_IN = B_Q_RANK + B_KV_RANK + B_ROPE + B_WIDTH

DEEPNORM_ALPHA = (2.0 * DEPTH) ** 0.25
DEEPNORM_BETA = (8.0 * DEPTH) ** -0.25
LN_EPS = 1e-5
RMS_EPS = 1e-6
SUBLN_EPS = 1e-5
NEG_INF = -1e30
POS_OFFSET_MAX_CHUNKS = 64

kernel_name = "hybrid_diffattn_mla_deepnorm_adaln"


def _layer_norm(x, g, b):
    xf = x.astype(jnp.float32)
    mu = jnp.mean(xf, axis=-1, keepdims=True)
    var = jnp.mean(jnp.square(xf - mu), axis=-1, keepdims=True)
    return ((xf - mu) * lax.rsqrt(var + LN_EPS) * g.astype(jnp.float32)
            + b.astype(jnp.float32)).astype(x.dtype)


def _rms_norm(x, g, eps):
    xf = x.astype(jnp.float32)
    ms = jnp.mean(jnp.square(xf), axis=-1, keepdims=True)
    return (xf * lax.rsqrt(ms + eps) * g.astype(jnp.float32)).astype(x.dtype)


def _rope_cos_sin(positions, rot_dim):
    inv_freq = ROPE_THETA ** (-jnp.arange(0, rot_dim, 2, dtype=jnp.float32) / rot_dim)
    ang = positions.astype(jnp.float32)[..., None] * inv_freq
    return jnp.cos(ang), jnp.sin(ang)


def _rotate(x, cos, sin):
    half = x.shape[-1] // 2
    x1 = x[..., :half].astype(jnp.float32)
    x2 = x[..., half:].astype(jnp.float32)
    return jnp.concatenate([x1 * cos - x2 * sin, x2 * cos + x1 * sin], axis=-1).astype(x.dtype)


def _chunk_mask(q_start, seq):
    q_chunk = (q_start + jnp.arange(Q_BLOCK)) // CHUNK
    k_chunk = jnp.arange(seq) // CHUNK
    return k_chunk[None, :] <= q_chunk[:, None]


def _sweep_query_blocks(block_fn, seq):
    n_blocks = seq // Q_BLOCK
    out = lax.map(block_fn, jnp.arange(n_blocks) * Q_BLOCK)
    out = jnp.moveaxis(out, 0, 1)
    return out.reshape(out.shape[0], n_blocks * Q_BLOCK, *out.shape[3:])


def _diff_attention_branch(u, cos, sin, w_in, lq1, lk1, lq2, lk2, subln_g, w_out, lambda_init):
    b, s, _ = u.shape
    proj = u @ w_in
    q, k, v, gate = jnp.split(proj, [A_QK, 2 * A_QK, 2 * A_QK + A_V], axis=-1)
    q = q.reshape(b, s, A_HEADS, 2, A_HEAD_DIM)
    k = k.reshape(b, s, A_HEADS, 2, A_HEAD_DIM)
    v = v.reshape(b, s, A_HEADS, 2 * A_HEAD_DIM)
    cs, sn = cos[:, :, None, None, :], sin[:, :, None, None, :]
    q = jnp.concatenate([_rotate(q[..., :A_ROT], cs, sn), q[..., A_ROT:]], axis=-1)
    k = jnp.concatenate([_rotate(k[..., :A_ROT], cs, sn), k[..., A_ROT:]], axis=-1)
    f32 = jnp.float32
    lam = (jnp.exp(jnp.sum(lq1.astype(f32) * lk1.astype(f32)))
           - jnp.exp(jnp.sum(lq2.astype(f32) * lk2.astype(f32))) + lambda_init)
    scale = A_HEAD_DIM ** -0.5

    def block(q_start):
        qb = lax.dynamic_slice_in_dim(q, q_start, Q_BLOCK, axis=1)
        sc = jnp.einsum('bqhmd,bkhmd->bhmqk', qb, k, preferred_element_type=f32) * scale
        p = jax.nn.softmax(jnp.where(_chunk_mask(q_start, s), sc, NEG_INF), axis=-1)
        attn = p[:, :, 0] - lam * p[:, :, 1]
        return jnp.einsum('bhqk,bkhe->bqhe', attn.astype(v.dtype), v)

    o = _sweep_query_blocks(block, s)
    o = _rms_norm(o, subln_g, SUBLN_EPS) * (1.0 - lambda_init)
    o = o.reshape(b, s, A_WIDTH) * jax.nn.silu(gate)
    return o @ w_out


def _mla_branch(u, cos, sin, w_in, q_norm_g, w_uq, kv_norm_g, w_ukv, w_out):
    b, s, _ = u.shape
    proj = u @ w_in
    q_lat, kv_lat, k_rope, gate = jnp.split(
        proj, [B_Q_RANK, B_Q_RANK + B_KV_RANK, B_Q_RANK + B_KV_RANK + B_ROPE], axis=-1)
    q = (_rms_norm(q_lat, q_norm_g, RMS_EPS) @ w_uq).reshape(b, s, B_HEADS, B_NOPE + B_ROPE)
    q_nope = q[..., :B_NOPE]
    q_rope = _rotate(q[..., B_NOPE:], cos[:, :, None], sin[:, :, None])
    kv = (_rms_norm(kv_lat, kv_norm_g, RMS_EPS) @ w_ukv).reshape(b, s, B_HEADS, B_NOPE + B_VDIM)
    k_nope, v = kv[..., :B_NOPE], kv[..., B_NOPE:]
    k_rope = _rotate(k_rope, cos, sin)
    f32 = jnp.float32
    scale = (B_NOPE + B_ROPE) ** -0.5

    def block(q_start):
        qn = lax.dynamic_slice_in_dim(q_nope, q_start, Q_BLOCK, axis=1)
        qr = lax.dynamic_slice_in_dim(q_rope, q_start, Q_BLOCK, axis=1)
        sc = (jnp.einsum('bqhd,bkhd->bhqk', qn, k_nope, preferred_element_type=f32)
              + jnp.einsum('bqhr,bkr->bhqk', qr, k_rope, preferred_element_type=f32)) * scale
        p = jax.nn.softmax(jnp.where(_chunk_mask(q_start, s), sc, NEG_INF), axis=-1)
        return jnp.einsum('bhqk,bkhd->bqhd', p.astype(v.dtype), v)

    o = _sweep_query_blocks(block, s)
    o = o.reshape(b, s, B_WIDTH) * jax.nn.silu(gate)
    return o @ w_out


def setup_inputs(seed: int = 0) -> dict:
    key = jax.random.key(seed)
    ks = jax.random.split(key, 24)
    nrm = lambda k, shape, sc: jax.random.normal(k, shape, jnp.float32) * sc
    x = nrm(ks[0], (BATCH, SEQ, D_MODEL), 1.0)
    c = nrm(ks[1], (BATCH, D_MODEL), 1.0)
    start = jax.random.randint(ks[2], (BATCH, 1), 0, POS_OFFSET_MAX_CHUNKS, dtype=jnp.int32) * CHUNK
    positions = (start + jnp.arange(SEQ, dtype=jnp.int32)[None, :]).astype(jnp.int32)
    return {
        "x": x,
        "c": c,
        "positions": positions,
        "ada_w": nrm(ks[3], (DEPTH, D_MODEL, 3 * D_MODEL), 0.5 * D_MODEL ** -0.5),
        "ada_b": nrm(ks[4], (DEPTH, 3 * D_MODEL), 0.02),
        "ln_g": 1.0 + nrm(ks[5], (DEPTH, D_MODEL), 0.02),
        "ln_b": nrm(ks[6], (DEPTH, D_MODEL), 0.02),
        "a_w_in": nrm(ks[7], (N_A_LAYERS, D_MODEL, A_IN), D_MODEL ** -0.5),
        "a_lambda_q1": nrm(ks[8], (N_A_LAYERS, A_HEAD_DIM), 0.1),
        "a_lambda_k1": nrm(ks[9], (N_A_LAYERS, A_HEAD_DIM), 0.1),
        "a_lambda_q2": nrm(ks[10], (N_A_LAYERS, A_HEAD_DIM), 0.1),
        "a_lambda_k2": nrm(ks[11], (N_A_LAYERS, A_HEAD_DIM), 0.1),
        "a_subln_g": 1.0 + nrm(ks[12], (N_A_LAYERS, 2 * A_HEAD_DIM), 0.02),
        "a_w_out": nrm(ks[13], (N_A_LAYERS, A_WIDTH, D_MODEL), DEEPNORM_BETA * A_WIDTH ** -0.5),
        "b_w_in": nrm(ks[14], (N_B_LAYERS, D_MODEL, B_IN), D_MODEL ** -0.5),
        "b_q_norm_g": 1.0 + nrm(ks[15], (N_B_LAYERS, B_Q_RANK), 0.02),
        "b_w_uq": nrm(ks[16], (N_B_LAYERS, B_Q_RANK, B_HEADS * (B_NOPE + B_ROPE)), B_Q_RANK ** -0.5),
        "b_kv_norm_g": 1.0 + nrm(ks[17], (N_B_LAYERS, B_KV_RANK), 0.02),
        "b_w_ukv": nrm(ks[18], (N_B_LAYERS, B_KV_RANK, B_HEADS * (B_NOPE + B_VDIM)), B_KV_RANK ** -0.5),
        "b_w_out": nrm(ks[19], (N_B_LAYERS, B_WIDTH, D_MODEL), DEEPNORM_BETA * B_WIDTH ** -0.5),
    }


def reference(x, c, positions, ada_w, ada_b, ln_g, ln_b,
              a_w_in, a_lambda_q1, a_lambda_k1, a_lambda_q2, a_lambda_k2, a_subln_g, a_w_out,
              b_w_in, b_q_norm_g, b_w_uq, b_kv_norm_g, b_w_ukv, b_w_out):
    cos_a, sin_a = _rope_cos_sin(positions, A_ROT)
    cos_b, sin_b = _rope_cos_sin(positions, B_ROPE)
    c_act = jax.nn.silu(c)
    for i in range(DEPTH):
        shift, scale, gate = jnp.split(c_act @ ada_w[i] + ada_b[i], 3, axis=-1)
        u = x * (1.0 + scale[:, None, :]) + shift[:, None, :]
        j = i // N_MIXERS
        if i % N_MIXERS == 0:
            lambda_init = 0.8 - 0.6 * math.exp(-0.3 * i)
            y = _diff_attention_branch(u, cos_a, sin_a, a_w_in[j], a_lambda_q1[j], a_lambda_k1[j],
                                       a_lambda_q2[j], a_lambda_k2[j], a_subln_g[j], a_w_out[j],
                                       lambda_init)
        else:
            y = _mla_branch(u, cos_b, sin_b, b_w_in[j], b_q_norm_g[j], b_w_uq[j],
                            b_kv_norm_g[j], b_w_ukv[j], b_w_out[j])
        x = _layer_norm(DEEPNORM_ALPHA * x + gate[:, None, :] * y, ln_g[i], ln_b[i])
    return x
```

```python
import functools
import math

import jax
import jax.numpy as jnp
import numpy as np
from jax import lax
from jax.experimental import pallas as pl
from jax.experimental.pallas import tpu as pltpu

F32 = jnp.float32
BF16 = jnp.bfloat16

D_MODEL = 1024
CHUNK = 64
ROPE_THETA = 500000.0

A_HEADS = 8
A_HEAD_DIM = 64
A_ROT = A_HEAD_DIM // 4
A_WIDTH = A_HEADS * 2 * A_HEAD_DIM

B_HEADS = 16
B_NOPE = 64
B_ROPE = 32
B_VDIM = 64
B_Q_RANK = 512
B_KV_RANK = 256
B_WIDTH = B_HEADS * B_VDIM

DEPTH = 2
DEEPNORM_ALPHA = (2.0 * DEPTH) ** 0.25
LN_EPS = 1e-5
RMS_EPS = 1e-6
SUBLN_EPS = 1e-5
NEG_INF = -1e30

LANES = 128
ROW_TILE = 512
Q_TILE = 256
KV_TILE = 256
VMEM_LIMIT = 56 * 1024 * 1024

_NT = (((1,), (1,)), ((), ()))


def _silu(x):
    return x * (1.0 / (1.0 + jnp.exp(-x)))


def _mod_kernel(c_ref, w_ref, b_ref, o_ref):
    ca = _silu(c_ref[...])
    o_ref[0] = jnp.dot(ca, w_ref[0], precision=lax.Precision.HIGHEST,
                       preferred_element_type=F32) + b_ref[0]


def _modulation(c, ada_w, ada_b):
    depth, d, n3 = ada_w.shape
    b = c.shape[0]
    tn = 1024
    return pl.pallas_call(
        _mod_kernel,
        grid=(depth, n3 // tn),
        in_specs=[pl.BlockSpec((b, d), lambda i, j: (0, 0)),
                  pl.BlockSpec((1, d, tn), lambda i, j: (i, 0, j)),
                  pl.BlockSpec((1, 1, tn), lambda i, j: (i, 0, j))],
        out_specs=pl.BlockSpec((1, b, tn), lambda i, j: (i, 0, j)),
        out_shape=jax.ShapeDtypeStruct((depth, b, n3), F32),
        compiler_params=pltpu.CompilerParams(vmem_limit_bytes=VMEM_LIMIT),
        name="modulation",
    )(c, ada_w, ada_b.reshape(depth, 1, n3))


def _rope_tables(pos_ref, row_tile_idx, consts_ref):
    invf = consts_ref[0:1, :]
    cs, sn = [], []
    for r in range(ROW_TILE // LANES):
        row = pos_ref[0, pl.ds(row_tile_idx * (ROW_TILE // LANES) + r, 1), :].astype(F32)
        pos_col = jnp.broadcast_to(row, (LANES, LANES)).T
        ang = pos_col * invf
        cs.append(jnp.cos(ang))
        sn.append(jnp.sin(ang))
    cos_t = jnp.concatenate(cs, axis=0)
    sin_t = jnp.concatenate(sn, axis=0)
    return cos_t, sin_t * consts_ref[1:2, :], sin_t * consts_ref[2:3, :]


def _rope_apply(a, cos_t, sin_up, sin_dn, half):
    return (a * cos_t + pltpu.roll(a, LANES - half, 1) * sin_up
            + pltpu.roll(a, half, 1) * sin_dn)


def _rope_consts(rot_dim, group, offset):
    inv_freq = ROPE_THETA ** (-jnp.arange(0, rot_dim, 2, dtype=F32) / rot_dim)
    half = rot_dim // 2
    lane = np.arange(LANES)
    d = lane % group - offset
    rotated = (d >= 0) & (d < rot_dim)
    first = rotated & (d < half)
    second = rotated & (d >= half)
    idx = np.where(rotated, d % half, 0)
    invf = jnp.where(jnp.asarray(rotated), inv_freq[idx], 0.0)
    rows = jnp.stack([invf, jnp.asarray(-first.astype(np.float32)),
                      jnp.asarray(second.astype(np.float32))])
    return jnp.concatenate([rows, jnp.zeros((5, LANES), F32)], axis=0)


def _a_inproj_kernel(x_ref, mod_ref, pos_ref, consts_ref, w_ref, q_ref, k_ref, v_ref, g_ref):
    i = pl.program_id(1)
    shift = mod_ref[0, 0:1, :]
    scale = mod_ref[0, 1:2, :]
    u = (x_ref[0] * (1.0 + scale) + shift).astype(BF16)
    cos_t, sin_up, sin_dn = _rope_tables(pos_ref, i, consts_ref)
    half = A_ROT // 2
    qk_scale = A_HEAD_DIM ** -0.5

    def rope_store(col0, o_ref, mult):
        acc = jnp.dot(u, w_ref[:, col0:col0 + A_WIDTH], preferred_element_type=F32)
        for g in range(A_WIDTH // LANES):
            a = acc[:, g * LANES:(g + 1) * LANES]
            r = _rope_apply(a, cos_t, sin_up, sin_dn, half)
            if mult != 1.0:
                r = r * mult
            o_ref[0, :, g * LANES:(g + 1) * LANES] = r.astype(BF16)

    rope_store(0, q_ref, qk_scale)
    rope_store(A_WIDTH, k_ref, 1.0)
    v_ref[0] = jnp.dot(u, w_ref[:, 2 * A_WIDTH:3 * A_WIDTH],
                       preferred_element_type=F32).astype(BF16)
    gate = jnp.dot(u, w_ref[:, 3 * A_WIDTH:4 * A_WIDTH], preferred_element_type=F32)
    g_ref[0] = _silu(gate).astype(BF16)


def _a_inproj(x, mod, pos3, consts, w_bf16):
    b, s, d = x.shape
    n = w_bf16.shape[1]
    out = jax.ShapeDtypeStruct((b, s, A_WIDTH), BF16)
    ospec = pl.BlockSpec((1, ROW_TILE, A_WIDTH), lambda bi, i: (bi, i, 0))
    return pl.pallas_call(
        _a_inproj_kernel,
        grid=(b, s // ROW_TILE),
        in_specs=[pl.BlockSpec((1, ROW_TILE, d), lambda bi, i: (bi, i, 0)),
                  pl.BlockSpec((1, 3, d), lambda bi, i: (bi, 0, 0)),
                  pl.BlockSpec((1, s // LANES, LANES), lambda bi, i: (bi, 0, 0)),
                  pl.BlockSpec((8, LANES), lambda bi, i: (0, 0)),
                  pl.BlockSpec((d, n), lambda bi, i: (0, 0))],
        out_specs=[ospec, ospec, ospec, ospec],
        out_shape=[out, out, out, out],
        compiler_params=pltpu.CompilerParams(vmem_limit_bytes=VMEM_LIMIT),
        name="a_inproj",
    )(x, mod, pos3, consts, w_bf16)


def _chunk_mask():
    krow = lax.broadcasted_iota(jnp.int32, (KV_TILE, Q_TILE), 0) // CHUNK
    qcol = lax.broadcasted_iota(jnp.int32, (KV_TILE, Q_TILE), 1) // CHUNK
    return krow <= qcol


def _online_step(state, s_t, v_t):
    m, l, acc = state
    m_new = jnp.maximum(m, jnp.max(s_t, axis=0, keepdims=True))
    alpha = jnp.exp(m - m_new)
    p = jnp.exp(s_t - m_new)
    l = alpha * l + jnp.sum(p, axis=0, keepdims=True)
    acc = alpha * acc + jnp.dot(v_t, p.astype(BF16), preferred_element_type=F32)
    return m_new, l, acc


def _init_state(dv):
    return (jnp.full((1, Q_TILE), NEG_INF, F32), jnp.zeros((1, Q_TILE), F32),
            jnp.zeros((dv, Q_TILE), F32))


def _attend(q, qi, k_at, v_at, dv, mask):
    def body(j, state):
        s_t = lax.dot_general(k_at(j), q, _NT, preferred_element_type=F32)
        return _online_step(state, s_t, v_at(j))

    state = lax.fori_loop(0, qi, body, _init_state(dv))
    s_t = lax.dot_general(k_at(qi), q, _NT, preferred_element_type=F32)
    s_t = jnp.where(mask, s_t, NEG_INF)
    return _online_step(state, s_t, v_at(qi))


def _store_transposed_values(v_ref, vt_ref, n_kv):
    for j in range(n_kv):
        blk = v_ref[0, j * KV_TILE:(j + 1) * KV_TILE, :].astype(F32)
        vt_ref[j] = blk.T.astype(BF16)


def _a_attn_kernel(lambda_init, q_ref, k_ref, v_ref, g_ref, lam_ref, subg_ref, o_ref,
                   kp_ref, vt_ref):
    s = q_ref.shape[1]
    n_kv = s // KV_TILE
    lane = lax.broadcasted_iota(jnp.int32, (KV_TILE, LANES), 1)
    first_map = lane < A_HEAD_DIM
    for j in range(n_kv):
        kb = k_ref[0, j * KV_TILE:(j + 1) * KV_TILE, :].astype(F32)
        kp_ref[0, j] = jnp.where(first_map, kb, 0.0).astype(BF16)
        kp_ref[1, j] = jnp.where(first_map, 0.0, kb).astype(BF16)
    _store_transposed_values(v_ref, vt_ref, n_kv)

    lam = (jnp.exp(jnp.sum(lam_ref[0:1, :] * lam_ref[1:2, :], axis=1, keepdims=True))
           - jnp.exp(jnp.sum(lam_ref[2:3, :] * lam_ref[3:4, :], axis=1, keepdims=True))
           + lambda_init)
    out_gain = subg_ref[0:1, :] * (1.0 - lambda_init)
    mask = _chunk_mask()
    dv = 2 * A_HEAD_DIM

    for qi in range(s // Q_TILE):
        rows = slice(qi * Q_TILE, (qi + 1) * Q_TILE)
        q = q_ref[0, rows, :]
        outs = []
        for mp in range(2):
            _, l, acc = _attend(q, qi, lambda j, mp=mp: kp_ref[mp, j], lambda j: vt_ref[j],
                                dv, mask)
            outs.append((l, acc))
        o_t = outs[0][1] * (1.0 / outs[0][0]) - outs[1][1] * (lam / outs[1][0])
        ms = jnp.mean(o_t * o_t, axis=0, keepdims=True)
        o_t = o_t * lax.rsqrt(ms + SUBLN_EPS)
        o = o_t.T * out_gain * g_ref[0, rows, :].astype(F32)
        o_ref[0, rows, :] = o.astype(BF16)


def _a_attention(q, k, v, g, lam_rows, subg_rows, lambda_init):
    b, s, _ = q.shape
    n_kv = s // KV_TILE
    spec = pl.BlockSpec((1, s, LANES), lambda bi, h: (bi, 0, h))
    small = pl.BlockSpec((8, LANES), lambda bi, h: (0, 0))
    return pl.pallas_call(
        functools.partial(_a_attn_kernel, lambda_init),
        grid=(b, A_HEADS),
        in_specs=[spec, spec, spec, spec, small, small],
        out_specs=spec,
        out_shape=jax.ShapeDtypeStruct((b, s, A_WIDTH), BF16),
        scratch_shapes=[pltpu.VMEM((2, n_kv, KV_TILE, LANES), BF16),
                        pltpu.VMEM((n_kv, LANES, KV_TILE), BF16)],
        compiler_params=pltpu.CompilerParams(vmem_limit_bytes=VMEM_LIMIT),
        name="a_attention",
    )(q, k, v, g, lam_rows, subg_rows)


def _outproj_ln_kernel(o_ref, w_ref, x_ref, mod_ref, lng_ref, lnb_ref, y_ref):
    y = jnp.dot(o_ref[0], w_ref[...], preferred_element_type=F32)
    gate = mod_ref[0, 2:3, :]
    z = DEEPNORM_ALPHA * x_ref[0] + gate * y
    mu = jnp.mean(z, axis=1, keepdims=True)
    zc = z - mu
    var = jnp.mean(zc * zc, axis=1, keepdims=True)
    y_ref[0] = zc * lax.rsqrt(var + LN_EPS) * lng_ref[...] + lnb_ref[...]


def _outproj_ln(o, w_bf16, x, mod, ln_g, ln_b):
    b, s, d = x.shape
    width = o.shape[2]
    row = pl.BlockSpec((1, d), lambda bi, i: (0, 0))
    return pl.pallas_call(
        _outproj_ln_kernel,
        grid=(b, s // ROW_TILE),
        in_specs=[pl.BlockSpec((1, ROW_TILE, width), lambda bi, i: (bi, i, 0)),
                  pl.BlockSpec((width, d), lambda bi, i: (0, 0)),
                  pl.BlockSpec((1, ROW_TILE, d), lambda bi, i: (bi, i, 0)),
                  pl.BlockSpec((1, 3, d), lambda bi, i: (bi, 0, 0)),
                  row, row],
        out_specs=pl.BlockSpec((1, ROW_TILE, d), lambda bi, i: (bi, i, 0)),
        out_shape=jax.ShapeDtypeStruct((b, s, d), F32),
        compiler_params=pltpu.CompilerParams(vmem_limit_bytes=VMEM_LIMIT),
        name="outproj_ln",
    )(o, w_bf16, x, mod, ln_g.reshape(1, d), ln_b.reshape(1, d))


B_HEAD_PAD = LANES
B_QK_PAD = B_HEADS * B_HEAD_PAD
B_COL_KV = B_Q_RANK
B_COL_ROPE = B_Q_RANK + B_KV_RANK
B_COL_GATE = B_COL_ROPE + LANES
B_IN_PAD = B_COL_GATE + B_WIDTH
B_COL_CHUNK = 512


def _rms(x, g_row, eps):
    ms = jnp.mean(x * x, axis=1, keepdims=True)
    return x * lax.rsqrt(ms + eps) * g_row


def _b_inproj_kernel(x_ref, mod_ref, pos_ref, consts_ref, w1_ref, qg_ref, wuq_ref, kvg_ref,
                     wk_ref, wv_ref, q_ref, k_ref, v_ref, g_ref):
    i = pl.program_id(1)
    shift = mod_ref[0, 0:1, :]
    scale = mod_ref[0, 1:2, :]
    u = (x_ref[0] * (1.0 + scale) + shift).astype(BF16)
    cos_t, sin_up, sin_dn = _rope_tables(pos_ref, i, consts_ref)
    half = B_ROPE // 2
    qk_scale = (B_NOPE + B_ROPE) ** -0.5

    lat = jnp.dot(u, w1_ref[:, 0:B_COL_GATE], preferred_element_type=F32)
    qn = _rms(lat[:, 0:B_Q_RANK], qg_ref[...], RMS_EPS).astype(BF16)
    kvn = _rms(lat[:, B_COL_KV:B_COL_ROPE], kvg_ref[...], RMS_EPS).astype(BF16)
    k_rope = _rope_apply(lat[:, B_COL_ROPE:B_COL_GATE], cos_t, sin_up, sin_dn, half)

    cos_q, up_q, dn_q = cos_t * qk_scale, sin_up * qk_scale, sin_dn * qk_scale
    for c0 in range(0, B_QK_PAD, B_COL_CHUNK):
        qc = jnp.dot(qn, wuq_ref[:, c0:c0 + B_COL_CHUNK], preferred_element_type=F32)
        kc = jnp.dot(kvn, wk_ref[:, c0:c0 + B_COL_CHUNK], preferred_element_type=F32)
        for g0 in range(0, B_COL_CHUNK, B_HEAD_PAD):
            cols = slice(c0 + g0, c0 + g0 + B_HEAD_PAD)
            q_ref[0, :, cols] = _rope_apply(qc[:, g0:g0 + B_HEAD_PAD], cos_q, up_q, dn_q,
                                            half).astype(BF16)
            k_ref[0, :, cols] = (kc[:, g0:g0 + B_HEAD_PAD] + k_rope).astype(BF16)
    v_ref[0] = jnp.dot(kvn, wv_ref[...], preferred_element_type=F32).astype(BF16)
    gate = jnp.dot(u, w1_ref[:, B_COL_GATE:B_IN_PAD], preferred_element_type=F32)
    g_ref[0] = _silu(gate).astype(BF16)


def _b_inproj(x, mod, pos3, consts, w1, qg, wuq, kvg, wk, wv):
    b, s, d = x.shape

    def full(a):
        return pl.BlockSpec(a.shape, lambda bi, i: (0,) * a.ndim)

    def rows(width):
        return pl.BlockSpec((1, ROW_TILE, width), lambda bi, i: (bi, i, 0))

    return pl.pallas_call(
        _b_inproj_kernel,
        grid=(b, s // ROW_TILE),
        in_specs=[rows(d),
                  pl.BlockSpec((1, 3, d), lambda bi, i: (bi, 0, 0)),
                  pl.BlockSpec((1, s // LANES, LANES), lambda bi, i: (bi, 0, 0)),
                  full(consts), full(w1), full(qg), full(wuq), full(kvg), full(wk), full(wv)],
        out_specs=[rows(B_QK_PAD), rows(B_QK_PAD), rows(B_WIDTH), rows(B_WIDTH)],
        out_shape=[jax.ShapeDtypeStruct((b, s, B_QK_PAD), BF16),
                   jax.ShapeDtypeStruct((b, s, B_QK_PAD), BF16),
                   jax.ShapeDtypeStruct((b, s, B_WIDTH), BF16),
                   jax.ShapeDtypeStruct((b, s, B_WIDTH), BF16)],
        compiler_params=pltpu.CompilerParams(vmem_limit_bytes=VMEM_LIMIT),
        name="b_inproj",
    )(x, mod, pos3, consts, w1, qg, wuq, kvg, wk, wv)


B_HEADS_PER_STEP = LANES // B_VDIM


def _b_attn_kernel(q_ref, k_ref, v_ref, g_ref, o_ref, vt_ref):
    s = q_ref.shape[1]
    n_kv = s // KV_TILE
    _store_transposed_values(v_ref, vt_ref, n_kv)
    mask = _chunk_mask()

    for qi in range(s // Q_TILE):
        rows = slice(qi * Q_TILE, (qi + 1) * Q_TILE)
        parts = []
        for hh in range(B_HEADS_PER_STEP):
            cols = slice(hh * B_HEAD_PAD, (hh + 1) * B_HEAD_PAD)
            vrows = slice(hh * B_VDIM, (hh + 1) * B_VDIM)
            q = q_ref[0, rows, cols]
            _, l, acc = _attend(
                q, qi,
                lambda j: k_ref[0, pl.ds(pl.multiple_of(j * KV_TILE, KV_TILE), KV_TILE), cols],
                lambda j: vt_ref[j, vrows, :], B_VDIM, mask)
            parts.append(acc * (1.0 / l))
        o = jnp.concatenate(parts, axis=0).T * g_ref[0, rows, :].astype(F32)
        o_ref[0, rows, :] = o.astype(BF16)


def _b_attention(q, k, v, g):
    b, s, _ = q.shape
    n_kv = s // KV_TILE
    qk_spec = pl.BlockSpec((1, s, B_HEADS_PER_STEP * B_HEAD_PAD), lambda bi, h: (bi, 0, h))
    spec = pl.BlockSpec((1, s, LANES), lambda bi, h: (bi, 0, h))
    return pl.pallas_call(
        _b_attn_kernel,
        grid=(b, B_HEADS // B_HEADS_PER_STEP),
        in_specs=[qk_spec, qk_spec, spec, spec],
        out_specs=spec,
        out_shape=jax.ShapeDtypeStruct((b, s, B_WIDTH), BF16),
        scratch_shapes=[pltpu.VMEM((n_kv, LANES, KV_TILE), BF16)],
        compiler_params=pltpu.CompilerParams(vmem_limit_bytes=VMEM_LIMIT),
        name="b_attention",
    )(q, k, v, g)


def _pad_rows8(rows):
    out = jnp.zeros((8, LANES), F32)
    for r, vec in enumerate(rows):
        out = out.at[r, :vec.shape[0]].set(vec.astype(F32))
    return out


def _b_layouts(w_in, w_uq, w_ukv):
    d = w_in.shape[0]
    rope_cols = jnp.zeros((d, LANES), F32).at[:, B_NOPE:B_NOPE + B_ROPE].set(
        w_in[:, B_COL_ROPE:B_COL_ROPE + B_ROPE])
    w1 = jnp.concatenate([w_in[:, :B_COL_ROPE], rope_cols, w_in[:, B_COL_ROPE + B_ROPE:]], axis=1)
    uq = w_uq.reshape(B_Q_RANK, B_HEADS, B_NOPE + B_ROPE)
    uq = jnp.pad(uq, ((0, 0), (0, 0), (0, B_HEAD_PAD - B_NOPE - B_ROPE)))
    ukv = w_ukv.reshape(B_KV_RANK, B_HEADS, B_NOPE + B_VDIM)
    wk = jnp.pad(ukv[:, :, :B_NOPE], ((0, 0), (0, 0), (0, B_HEAD_PAD - B_NOPE)))
    wv = ukv[:, :, B_NOPE:]
    return (w1.astype(BF16), uq.reshape(B_Q_RANK, B_QK_PAD).astype(BF16),
            wk.reshape(B_KV_RANK, B_QK_PAD).astype(BF16),
            wv.reshape(B_KV_RANK, B_WIDTH).astype(BF16))


def kernel(x, c, positions, ada_w, ada_b, ln_g, ln_b, a_w_in, a_lambda_q1, a_lambda_k1,
           a_lambda_q2, a_lambda_k2, a_subln_g, a_w_out, b_w_in, b_q_norm_g, b_w_uq,
           b_kv_norm_g, b_w_ukv, b_w_out):
    b, s, d = x.shape
    assert d == D_MODEL and s % ROW_TILE == 0 and s % Q_TILE == 0 and Q_TILE == KV_TILE
    mod = _modulation(c, ada_w, ada_b).reshape(DEPTH, b, 3, d)
    pos3 = positions.reshape(b, s // LANES, LANES)

    lambda_init = 0.8 - 0.6 * math.exp(-0.3 * 0)
    q, k, v, g = _a_inproj(x, mod[0], pos3, _rope_consts(A_ROT, A_HEAD_DIM, 0),
                           a_w_in[0].astype(BF16))
    lam_rows = _pad_rows8([a_lambda_q1[0], a_lambda_k1[0], a_lambda_q2[0], a_lambda_k2[0]])
    o = _a_attention(q, k, v, g, lam_rows, _pad_rows8([a_subln_g[0]]), lambda_init)
    x = _outproj_ln(o, a_w_out[0].astype(BF16), x, mod[0], ln_g[0], ln_b[0])

    w1, wuq, wk, wv = _b_layouts(b_w_in[0], b_w_uq[0], b_w_ukv[0])
    q, k, v, g = _b_inproj(x, mod[1], pos3, _rope_consts(B_ROPE, B_HEAD_PAD, B_NOPE), w1,
                           b_q_norm_g[0].reshape(1, B_Q_RANK), wuq,
                           b_kv_norm_g[0].reshape(1, B_KV_RANK), wk, wv)
    o = _b_attention(q, k, v, g)
    return _outproj_ln(o, b_w_out[0].astype(BF16), x, mod[1], ln_g[1], ln_b[1])
```

```python
import functools
import math

import jax
import jax.numpy as jnp
import numpy as np
from jax import lax
from jax.experimental import pallas as pl
from jax.experimental.pallas import tpu as pltpu

F32 = jnp.float32
BF16 = jnp.bfloat16

D_MODEL = 1024
CHUNK = 64
ROPE_THETA = 500000.0

A_HEADS = 8
A_HEAD_DIM = 64
A_ROT = A_HEAD_DIM // 4
A_WIDTH = A_HEADS * 2 * A_HEAD_DIM

B_HEADS = 16
B_NOPE = 64
B_ROPE = 32
B_VDIM = 64
B_Q_RANK = 512
B_KV_RANK = 256
B_WIDTH = B_HEADS * B_VDIM

DEPTH = 2
DEEPNORM_ALPHA = (2.0 * DEPTH) ** 0.25
LN_EPS = 1e-5
RMS_EPS = 1e-6
SUBLN_EPS = 1e-5
NEG_INF = -1e30

LANES = 128
ROW_TILE = 512
Q_TILE = 256
KV_TILE = 256
SCORE_ROWS = 512
SCORE_SLOTS = 4
LOG2E = math.log2(math.e)
VMEM_LIMIT = 56 * 1024 * 1024

_NT = (((1,), (1,)), ((), ()))


def _silu(x):
    return x * (1.0 / (1.0 + jnp.exp(-x)))


def _mod_kernel(c_ref, w_ref, b_ref, o_ref):
    ca = _silu(c_ref[...])
    o_ref[0] = jnp.dot(ca, w_ref[0], precision=lax.Precision.HIGHEST,
                       preferred_element_type=F32) + b_ref[0]


def _modulation(c, ada_w, ada_b):
    depth, d, n3 = ada_w.shape
    b = c.shape[0]
    tn = 1024
    return pl.pallas_call(
        _mod_kernel,
        grid=(depth, n3 // tn),
        in_specs=[pl.BlockSpec((b, d), lambda i, j: (0, 0)),
                  pl.BlockSpec((1, d, tn), lambda i, j: (i, 0, j)),
                  pl.BlockSpec((1, 1, tn), lambda i, j: (i, 0, j))],
        out_specs=pl.BlockSpec((1, b, tn), lambda i, j: (i, 0, j)),
        out_shape=jax.ShapeDtypeStruct((depth, b, n3), F32),
        compiler_params=pltpu.CompilerParams(vmem_limit_bytes=VMEM_LIMIT),
        name="modulation",
    )(c, ada_w, ada_b.reshape(depth, 1, n3))


def _rope_tables(pos_ref, row_tile_idx, consts_ref):
    invf = consts_ref[0:1, :]
    cs, sn = [], []
    for r in range(ROW_TILE // LANES):
        row = pos_ref[0, pl.ds(row_tile_idx * (ROW_TILE // LANES) + r, 1), :].astype(F32)
        pos_col = jnp.broadcast_to(row, (LANES, LANES)).T
        ang = pos_col * invf
        cs.append(jnp.cos(ang))
        sn.append(jnp.sin(ang))
    cos_t = jnp.concatenate(cs, axis=0)
    sin_t = jnp.concatenate(sn, axis=0)
    return cos_t, sin_t * consts_ref[1:2, :], sin_t * consts_ref[2:3, :]


def _rope_apply(a, cos_t, sin_up, sin_dn, half):
    return (a * cos_t + pltpu.roll(a, LANES - half, 1) * sin_up
            + pltpu.roll(a, half, 1) * sin_dn)


def _rope_consts(rot_dim, group, offset):
    inv_freq = ROPE_THETA ** (-jnp.arange(0, rot_dim, 2, dtype=F32) / rot_dim)
    half = rot_dim // 2
    lane = np.arange(LANES)
    d = lane % group - offset
    rotated = (d >= 0) & (d < rot_dim)
    first = rotated & (d < half)
    second = rotated & (d >= half)
    idx = np.where(rotated, d % half, 0)
    invf = jnp.where(jnp.asarray(rotated), inv_freq[idx], 0.0)
    rows = jnp.stack([invf, jnp.asarray(-first.astype(np.float32)),
                      jnp.asarray(second.astype(np.float32))])
    return jnp.concatenate([rows, jnp.zeros((5, LANES), F32)], axis=0)


def _a_inproj_kernel(x_ref, mod_ref, pos_ref, consts_ref, w_ref, q_ref, k_ref, v_ref, g_ref):
    i = pl.program_id(1)
    shift = mod_ref[0, 0:1, :]
    scale = mod_ref[0, 1:2, :]
    u = (x_ref[0] * (1.0 + scale) + shift).astype(BF16)
    cos_t, sin_up, sin_dn = _rope_tables(pos_ref, i, consts_ref)
    half = A_ROT // 2
    qk_scale = A_HEAD_DIM ** -0.5 * LOG2E

    def rope_store(col0, o_ref, mult):
        acc = jnp.dot(u, w_ref[:, col0:col0 + A_WIDTH], preferred_element_type=F32)
        for g in range(A_WIDTH // LANES):
            a = acc[:, g * LANES:(g + 1) * LANES]
            r = _rope_apply(a, cos_t, sin_up, sin_dn, half)
            if mult != 1.0:
                r = r * mult
            o_ref[0, :, g * LANES:(g + 1) * LANES] = r.astype(BF16)

    rope_store(0, q_ref, qk_scale)
    rope_store(A_WIDTH, k_ref, 1.0)
    v_ref[0] = jnp.dot(u, w_ref[:, 2 * A_WIDTH:3 * A_WIDTH],
                       preferred_element_type=F32).astype(BF16)
    gate = jnp.dot(u, w_ref[:, 3 * A_WIDTH:4 * A_WIDTH], preferred_element_type=F32)
    g_ref[0] = _silu(gate).astype(BF16)


def _a_inproj(x, mod, pos3, consts, w_bf16):
    b, s, d = x.shape
    n = w_bf16.shape[1]
    out = jax.ShapeDtypeStruct((b, s, A_WIDTH), BF16)
    ospec = pl.BlockSpec((1, ROW_TILE, A_WIDTH), lambda bi, i: (bi, i, 0))
    return pl.pallas_call(
        _a_inproj_kernel,
        grid=(b, s // ROW_TILE),
        in_specs=[pl.BlockSpec((1, ROW_TILE, d), lambda bi, i: (bi, i, 0)),
                  pl.BlockSpec((1, 3, d), lambda bi, i: (bi, 0, 0)),
                  pl.BlockSpec((1, s // LANES, LANES), lambda bi, i: (bi, 0, 0)),
                  pl.BlockSpec((8, LANES), lambda bi, i: (0, 0)),
                  pl.BlockSpec((d, n), lambda bi, i: (0, 0))],
        out_specs=[ospec, ospec, ospec, ospec],
        out_shape=[out, out, out, out],
        compiler_params=pltpu.CompilerParams(vmem_limit_bytes=VMEM_LIMIT),
        name="a_inproj",
    )(x, mod, pos3, consts, w_bf16)


def _chunk_mask():
    krow = lax.broadcasted_iota(jnp.int32, (KV_TILE, Q_TILE), 0) // CHUNK
    qcol = lax.broadcasted_iota(jnp.int32, (KV_TILE, Q_TILE), 1) // CHUNK
    return krow <= qcol


def _attend(q, k_rows, vt_cols, kv_len, s_ref, p_ref, mask):
    diag = kv_len - KV_TILE
    s_d = lax.dot_general(k_rows(diag, kv_len), q, _NT, preferred_element_type=F32)
    s_d = jnp.where(mask, s_d, NEG_INF)
    s_ref[diag:kv_len, :] = s_d
    m = jnp.max(s_d, axis=0, keepdims=True)
    for r0 in range(0, diag, SCORE_ROWS):
        r1 = min(r0 + SCORE_ROWS, diag)
        s_o = lax.dot_general(k_rows(r0, r1), q, _NT, preferred_element_type=F32)
        s_ref[r0:r1, :] = s_o
        m = jnp.maximum(m, jnp.max(s_o, axis=0, keepdims=True))
    l = jnp.zeros((1, Q_TILE), F32)
    for r0 in range(0, kv_len, KV_TILE):
        p = jnp.exp2(s_ref[r0:r0 + KV_TILE, :] - m)
        l = l + jnp.sum(p, axis=0, keepdims=True)
        p_ref[r0:r0 + KV_TILE, :] = p.astype(BF16)
    acc = jnp.dot(vt_cols(0, kv_len), p_ref[0:kv_len, :], preferred_element_type=F32)
    return l, acc


def _store_transposed_values(v_ref, vt_ref):
    for r0 in range(0, v_ref.shape[1], KV_TILE):
        blk = v_ref[0, r0:r0 + KV_TILE, :].astype(F32)
        vt_ref[:, r0:r0 + KV_TILE] = blk.T.astype(BF16)


def _a_attn_kernel(lambda_init, q_ref, k_ref, v_ref, g_ref, lam_ref, subg_ref, o_ref,
                   kp_ref, vt_ref, s_ref, p_ref):
    s = q_ref.shape[1]
    lane = lax.broadcasted_iota(jnp.int32, (KV_TILE, LANES), 1)
    first_map = lane < A_HEAD_DIM
    for r0 in range(0, s, KV_TILE):
        kb = k_ref[0, r0:r0 + KV_TILE, :].astype(F32)
        kp_ref[0, r0:r0 + KV_TILE, :] = jnp.where(first_map, kb, 0.0).astype(BF16)
        kp_ref[1, r0:r0 + KV_TILE, :] = jnp.where(first_map, 0.0, kb).astype(BF16)
    _store_transposed_values(v_ref, vt_ref)

    lam = (jnp.exp(jnp.sum(lam_ref[0:1, :] * lam_ref[1:2, :], axis=1, keepdims=True))
           - jnp.exp(jnp.sum(lam_ref[2:3, :] * lam_ref[3:4, :], axis=1, keepdims=True))
           + lambda_init)
    out_gain = subg_ref[0:1, :] * (1.0 - lambda_init)
    mask = _chunk_mask()
    dv = 2 * A_HEAD_DIM

    for qi in range(s // Q_TILE):
        rows = slice(qi * Q_TILE, (qi + 1) * Q_TILE)
        q = q_ref[0, rows, :]
        outs = []
        for mp in range(2):
            slot = (qi % 2) * 2 + mp
            outs.append(_attend(q, lambda r0, r1, mp=mp: kp_ref[mp, r0:r1, :],
                                lambda c0, c1: vt_ref[:, c0:c1], (qi + 1) * Q_TILE,
                                s_ref.at[slot], p_ref.at[slot], mask))
        (l1, acc1), (l2, acc2) = outs
        o_t = acc1 * (1.0 / l1) - acc2 * (lam / l2)
        ms = jnp.mean(o_t * o_t, axis=0, keepdims=True)
        o_t = o_t * lax.rsqrt(ms + SUBLN_EPS)
        o = o_t.T * out_gain * g_ref[0, rows, :].astype(F32)
        o_ref[0, rows, :] = o.astype(BF16)


def _a_attention(q, k, v, g, lam_rows, subg_rows, lambda_init):
    b, s, _ = q.shape
    spec = pl.BlockSpec((1, s, LANES), lambda bi, h: (bi, 0, h))
    small = pl.BlockSpec((8, LANES), lambda bi, h: (0, 0))
    return pl.pallas_call(
        functools.partial(_a_attn_kernel, lambda_init),
        grid=(b, A_HEADS),
        in_specs=[spec, spec, spec, spec, small, small],
        out_specs=spec,
        out_shape=jax.ShapeDtypeStruct((b, s, A_WIDTH), BF16),
        scratch_shapes=[pltpu.VMEM((2, s, LANES), BF16),
                        pltpu.VMEM((LANES, s), BF16),
                        pltpu.VMEM((SCORE_SLOTS, s, Q_TILE), F32),
                        pltpu.VMEM((SCORE_SLOTS, s, Q_TILE), BF16)],
        compiler_params=pltpu.CompilerParams(vmem_limit_bytes=VMEM_LIMIT),
        name="a_attention",
    )(q, k, v, g, lam_rows, subg_rows)


def _outproj_ln_kernel(o_ref, w_ref, x_ref, mod_ref, lng_ref, lnb_ref, y_ref):
    y = jnp.dot(o_ref[0], w_ref[...], preferred_element_type=F32)
    gate = mod_ref[0, 2:3, :]
    z = DEEPNORM_ALPHA * x_ref[0] + gate * y
    mu = jnp.mean(z, axis=1, keepdims=True)
    zc = z - mu
    var = jnp.mean(zc * zc, axis=1, keepdims=True)
    y_ref[0] = zc * lax.rsqrt(var + LN_EPS) * lng_ref[...] + lnb_ref[...]


def _outproj_ln(o, w_bf16, x, mod, ln_g, ln_b):
    b, s, d = x.shape
    width = o.shape[2]
    row = pl.BlockSpec((1, d), lambda bi, i: (0, 0))
    return pl.pallas_call(
        _outproj_ln_kernel,
        grid=(b, s // ROW_TILE),
        in_specs=[pl.BlockSpec((1, ROW_TILE, width), lambda bi, i: (bi, i, 0)),
                  pl.BlockSpec((width, d), lambda bi, i: (0, 0)),
                  pl.BlockSpec((1, ROW_TILE, d), lambda bi, i: (bi, i, 0)),
                  pl.BlockSpec((1, 3, d), lambda bi, i: (bi, 0, 0)),
                  row, row],
        out_specs=pl.BlockSpec((1, ROW_TILE, d), lambda bi, i: (bi, i, 0)),
        out_shape=jax.ShapeDtypeStruct((b, s, d), F32),
        compiler_params=pltpu.CompilerParams(vmem_limit_bytes=VMEM_LIMIT),
        name="outproj_ln",
    )(o, w_bf16, x, mod, ln_g.reshape(1, d), ln_b.reshape(1, d))


B_HEAD_PAD = LANES
B_QK_PAD = B_HEADS * B_HEAD_PAD
B_COL_KV = B_Q_RANK
B_COL_ROPE = B_Q_RANK + B_KV_RANK
B_COL_GATE = B_COL_ROPE + LANES
B_IN_PAD = B_COL_GATE + B_WIDTH
B_COL_CHUNK = 512


def _rms(x, g_row, eps):
    ms = jnp.mean(x * x, axis=1, keepdims=True)
    return x * lax.rsqrt(ms + eps) * g_row


def _b_inproj_kernel(x_ref, mod_ref, pos_ref, consts_ref, w1_ref, qg_ref, wuq_ref, kvg_ref,
                     wk_ref, wv_ref, q_ref, k_ref, v_ref, g_ref):
    i = pl.program_id(1)
    shift = mod_ref[0, 0:1, :]
    scale = mod_ref[0, 1:2, :]
    u = (x_ref[0] * (1.0 + scale) + shift).astype(BF16)
    cos_t, sin_up, sin_dn = _rope_tables(pos_ref, i, consts_ref)
    half = B_ROPE // 2
    qk_scale = (B_NOPE + B_ROPE) ** -0.5 * LOG2E

    lat = jnp.dot(u, w1_ref[:, 0:B_COL_GATE], preferred_element_type=F32)
    qn = _rms(lat[:, 0:B_Q_RANK], qg_ref[...], RMS_EPS).astype(BF16)
    kvn = _rms(lat[:, B_COL_KV:B_COL_ROPE], kvg_ref[...], RMS_EPS).astype(BF16)
    k_rope = _rope_apply(lat[:, B_COL_ROPE:B_COL_GATE], cos_t, sin_up, sin_dn, half)

    cos_q, up_q, dn_q = cos_t * qk_scale, sin_up * qk_scale, sin_dn * qk_scale
    for c0 in range(0, B_QK_PAD, B_COL_CHUNK):
        qc = jnp.dot(qn, wuq_ref[:, c0:c0 + B_COL_CHUNK], preferred_element_type=F32)
        kc = jnp.dot(kvn, wk_ref[:, c0:c0 + B_COL_CHUNK], preferred_element_type=F32)
        for g0 in range(0, B_COL_CHUNK, B_HEAD_PAD):
            cols = slice(c0 + g0, c0 + g0 + B_HEAD_PAD)
            q_ref[0, :, cols] = _rope_apply(qc[:, g0:g0 + B_HEAD_PAD], cos_q, up_q, dn_q,
                                            half).astype(BF16)
            k_ref[0, :, cols] = (kc[:, g0:g0 + B_HEAD_PAD] + k_rope).astype(BF16)
    v_ref[0] = jnp.dot(kvn, wv_ref[...], preferred_element_type=F32).astype(BF16)
    gate = jnp.dot(u, w1_ref[:, B_COL_GATE:B_IN_PAD], preferred_element_type=F32)
    g_ref[0] = _silu(gate).astype(BF16)


def _b_inproj(x, mod, pos3, consts, w1, qg, wuq, kvg, wk, wv):
    b, s, d = x.shape

    def full(a):
        return pl.BlockSpec(a.shape, lambda bi, i: (0,) * a.ndim)

    def rows(width):
        return pl.BlockSpec((1, ROW_TILE, width), lambda bi, i: (bi, i, 0))

    return pl.pallas_call(
        _b_inproj_kernel,
        grid=(b, s // ROW_TILE),
        in_specs=[rows(d),
                  pl.BlockSpec((1, 3, d), lambda bi, i: (bi, 0, 0)),
                  pl.BlockSpec((1, s // LANES, LANES), lambda bi, i: (bi, 0, 0)),
                  full(consts), full(w1), full(qg), full(wuq), full(kvg), full(wk), full(wv)],
        out_specs=[rows(B_QK_PAD), rows(B_QK_PAD), rows(B_WIDTH), rows(B_WIDTH)],
        out_shape=[jax.ShapeDtypeStruct((b, s, B_QK_PAD), BF16),
                   jax.ShapeDtypeStruct((b, s, B_QK_PAD), BF16),
                   jax.ShapeDtypeStruct((b, s, B_WIDTH), BF16),
                   jax.ShapeDtypeStruct((b, s, B_WIDTH), BF16)],
        compiler_params=pltpu.CompilerParams(vmem_limit_bytes=VMEM_LIMIT),
        name="b_inproj",
    )(x, mod, pos3, consts, w1, qg, wuq, kvg, wk, wv)


B_HEADS_PER_STEP = LANES // B_VDIM


def _b_attn_kernel(q_ref, k_ref, v_ref, g_ref, o_ref, vt_ref, s_ref, p_ref):
    s = q_ref.shape[1]
    _store_transposed_values(v_ref, vt_ref)
    mask = _chunk_mask()

    for qi in range(s // Q_TILE):
        rows = slice(qi * Q_TILE, (qi + 1) * Q_TILE)
        parts = []
        for hh in range(B_HEADS_PER_STEP):
            cols = slice(hh * B_HEAD_PAD, (hh + 1) * B_HEAD_PAD)
            vrows = slice(hh * B_VDIM, (hh + 1) * B_VDIM)
            slot = (qi % 2) * B_HEADS_PER_STEP + hh
            l, acc = _attend(q_ref[0, rows, cols],
                             lambda r0, r1, cols=cols: k_ref[0, r0:r1, cols],
                             lambda c0, c1, vrows=vrows: vt_ref[vrows, c0:c1],
                             (qi + 1) * Q_TILE, s_ref.at[slot], p_ref.at[slot], mask)
            parts.append(acc * (1.0 / l))
        o = jnp.concatenate(parts, axis=0).T * g_ref[0, rows, :].astype(F32)
        o_ref[0, rows, :] = o.astype(BF16)


def _b_attention(q, k, v, g):
    b, s, _ = q.shape
    assert SCORE_SLOTS == 2 * B_HEADS_PER_STEP
    qk_spec = pl.BlockSpec((1, s, B_HEADS_PER_STEP * B_HEAD_PAD), lambda bi, h: (bi, 0, h))
    spec = pl.BlockSpec((1, s, LANES), lambda bi, h: (bi, 0, h))
    return pl.pallas_call(
        _b_attn_kernel,
        grid=(b, B_HEADS // B_HEADS_PER_STEP),
        in_specs=[qk_spec, qk_spec, spec, spec],
        out_specs=spec,
        out_shape=jax.ShapeDtypeStruct((b, s, B_WIDTH), BF16),
        scratch_shapes=[pltpu.VMEM((LANES, s), BF16),
                        pltpu.VMEM((SCORE_SLOTS, s, Q_TILE), F32),
                        pltpu.VMEM((SCORE_SLOTS, s, Q_TILE), BF16)],
        compiler_params=pltpu.CompilerParams(vmem_limit_bytes=VMEM_LIMIT),
        name="b_attention",
    )(q, k, v, g)


def _pad_rows8(rows):
    out = jnp.zeros((8, LANES), F32)
    for r, vec in enumerate(rows):
        out = out.at[r, :vec.shape[0]].set(vec.astype(F32))
    return out


def _b_layouts(w_in, w_uq, w_ukv):
    d = w_in.shape[0]
    rope_cols = jnp.zeros((d, LANES), F32).at[:, B_NOPE:B_NOPE + B_ROPE].set(
        w_in[:, B_COL_ROPE:B_COL_ROPE + B_ROPE])
    w1 = jnp.concatenate([w_in[:, :B_COL_ROPE], rope_cols, w_in[:, B_COL_ROPE + B_ROPE:]], axis=1)
    uq = w_uq.reshape(B_Q_RANK, B_HEADS, B_NOPE + B_ROPE)
    uq = jnp.pad(uq, ((0, 0), (0, 0), (0, B_HEAD_PAD - B_NOPE - B_ROPE)))
    ukv = w_ukv.reshape(B_KV_RANK, B_HEADS, B_NOPE + B_VDIM)
    wk = jnp.pad(ukv[:, :, :B_NOPE], ((0, 0), (0, 0), (0, B_HEAD_PAD - B_NOPE)))
    wv = ukv[:, :, B_NOPE:]
    return (w1.astype(BF16), uq.reshape(B_Q_RANK, B_QK_PAD).astype(BF16),
            wk.reshape(B_KV_RANK, B_QK_PAD).astype(BF16),
            wv.reshape(B_KV_RANK, B_WIDTH).astype(BF16))


def kernel(x, c, positions, ada_w, ada_b, ln_g, ln_b, a_w_in, a_lambda_q1, a_lambda_k1,
           a_lambda_q2, a_lambda_k2, a_subln_g, a_w_out, b_w_in, b_q_norm_g, b_w_uq,
           b_kv_norm_g, b_w_ukv, b_w_out):
    b, s, d = x.shape
    assert d == D_MODEL and s % ROW_TILE == 0 and s % Q_TILE == 0 and Q_TILE == KV_TILE
    mod = _modulation(c, ada_w, ada_b).reshape(DEPTH, b, 3, d)
    pos3 = positions.reshape(b, s // LANES, LANES)

    lambda_init = 0.8 - 0.6 * math.exp(-0.3 * 0)
    q, k, v, g = _a_inproj(x, mod[0], pos3, _rope_consts(A_ROT, A_HEAD_DIM, 0),
                           a_w_in[0].astype(BF16))
    lam_rows = _pad_rows8([a_lambda_q1[0], a_lambda_k1[0], a_lambda_q2[0], a_lambda_k2[0]])
    o = _a_attention(q, k, v, g, lam_rows, _pad_rows8([a_subln_g[0]]), lambda_init)
    x = _outproj_ln(o, a_w_out[0].astype(BF16), x, mod[0], ln_g[0], ln_b[0])

    w1, wuq, wk, wv = _b_layouts(b_w_in[0], b_w_uq[0], b_w_ukv[0])
    q, k, v, g = _b_inproj(x, mod[1], pos3, _rope_consts(B_ROPE, B_HEAD_PAD, B_NOPE), w1,
                           b_q_norm_g[0].reshape(1, B_Q_RANK), wuq,
                           b_kv_norm_g[0].reshape(1, B_KV_RANK), wk, wv)
    o = _b_attention(q, k, v, g)
    return _outproj_ln(o, b_w_out[0].astype(BF16), x, mod[1], ln_g[1], ln_b[1])
```

```python
import functools
import math

import jax
import jax.numpy as jnp
import numpy as np
from jax import lax
from jax.experimental import pallas as pl
from jax.experimental.pallas import tpu as pltpu

F32 = jnp.float32
BF16 = jnp.bfloat16

D_MODEL = 1024
CHUNK = 64
ROPE_THETA = 500000.0

A_HEADS = 8
A_HEAD_DIM = 64
A_ROT = A_HEAD_DIM // 4
A_WIDTH = A_HEADS * 2 * A_HEAD_DIM

B_HEADS = 16
B_NOPE = 64
B_ROPE = 32
B_VDIM = 64
B_Q_RANK = 512
B_KV_RANK = 256
B_WIDTH = B_HEADS * B_VDIM

DEPTH = 2
DEEPNORM_ALPHA = (2.0 * DEPTH) ** 0.25
LN_EPS = 1e-5
RMS_EPS = 1e-6
SUBLN_EPS = 1e-5
NEG_INF = -1e30

LANES = 128
ROW_TILE = 512
Q_TILE = 256
KV_TILE = 256
SCORE_ROWS = 512
SCORE_SLOTS = 4
LOG2E = math.log2(math.e)
VMEM_LIMIT = 56 * 1024 * 1024

_NT = (((1,), (1,)), ((), ()))


def _silu(x):
    return x * (1.0 / (1.0 + jnp.exp(-x)))


def _mod_kernel(c_ref, w_ref, b_ref, o_ref):
    ca = _silu(c_ref[...])
    o_ref[0] = jnp.dot(ca, w_ref[0], precision=lax.Precision.HIGHEST,
                       preferred_element_type=F32) + b_ref[0]


def _modulation(c, ada_w, ada_b):
    depth, d, n3 = ada_w.shape
    b = c.shape[0]
    tn = 1024
    return pl.pallas_call(
        _mod_kernel,
        grid=(depth, n3 // tn),
        in_specs=[pl.BlockSpec((b, d), lambda i, j: (0, 0)),
                  pl.BlockSpec((1, d, tn), lambda i, j: (i, 0, j)),
                  pl.BlockSpec((1, 1, tn), lambda i, j: (i, 0, j))],
        out_specs=pl.BlockSpec((1, b, tn), lambda i, j: (i, 0, j)),
        out_shape=jax.ShapeDtypeStruct((depth, b, n3), F32),
        compiler_params=pltpu.CompilerParams(vmem_limit_bytes=VMEM_LIMIT),
        name="modulation",
    )(c, ada_w, ada_b.reshape(depth, 1, n3))


def _rope_tables(pos_ref, row_tile_idx, consts_ref):
    invf = consts_ref[0:1, :]
    cs, sn = [], []
    for r in range(ROW_TILE // LANES):
        row = pos_ref[0, pl.ds(row_tile_idx * (ROW_TILE // LANES) + r, 1), :].astype(F32)
        pos_col = jnp.broadcast_to(row, (LANES, LANES)).T
        ang = pos_col * invf
        cs.append(jnp.cos(ang))
        sn.append(jnp.sin(ang))
    cos_t = jnp.concatenate(cs, axis=0)
    sin_t = jnp.concatenate(sn, axis=0)
    return cos_t, sin_t * consts_ref[1:2, :], sin_t * consts_ref[2:3, :]


def _rope_apply(a, cos_t, sin_up, sin_dn, half):
    return (a * cos_t + pltpu.roll(a, LANES - half, 1) * sin_up
            + pltpu.roll(a, half, 1) * sin_dn)


def _rope_consts(rot_dim, group, offset):
    inv_freq = ROPE_THETA ** (-jnp.arange(0, rot_dim, 2, dtype=F32) / rot_dim)
    half = rot_dim // 2
    lane = np.arange(LANES)
    d = lane % group - offset
    rotated = (d >= 0) & (d < rot_dim)
    first = rotated & (d < half)
    second = rotated & (d >= half)
    idx = np.where(rotated, d % half, 0)
    invf = jnp.where(jnp.asarray(rotated), inv_freq[idx], 0.0)
    rows = jnp.stack([invf, jnp.asarray(-first.astype(np.float32)),
                      jnp.asarray(second.astype(np.float32))])
    return jnp.concatenate([rows, jnp.zeros((5, LANES), F32)], axis=0)


def _a_inproj_kernel(x_ref, mod_ref, pos_ref, consts_ref, w_ref, q_ref, k_ref, v_ref, g_ref):
    i = pl.program_id(1)
    shift = mod_ref[0, 0:1, :]
    scale = mod_ref[0, 1:2, :]
    u = (x_ref[0] * (1.0 + scale) + shift).astype(BF16)
    cos_t, sin_up, sin_dn = _rope_tables(pos_ref, i, consts_ref)
    half = A_ROT // 2
    qk_scale = A_HEAD_DIM ** -0.5 * LOG2E

    def rope_store(col0, o_ref, mult):
        acc = jnp.dot(u, w_ref[:, col0:col0 + A_WIDTH], preferred_element_type=F32)
        for g in range(A_WIDTH // LANES):
            a = acc[:, g * LANES:(g + 1) * LANES]
            r = _rope_apply(a, cos_t, sin_up, sin_dn, half)
            if mult != 1.0:
                r = r * mult
            o_ref[0, :, g * LANES:(g + 1) * LANES] = r.astype(BF16)

    rope_store(0, q_ref, qk_scale)
    rope_store(A_WIDTH, k_ref, 1.0)
    v_ref[0] = jnp.dot(u, w_ref[:, 2 * A_WIDTH:3 * A_WIDTH],
                       preferred_element_type=F32).astype(BF16)
    gate = jnp.dot(u, w_ref[:, 3 * A_WIDTH:4 * A_WIDTH], preferred_element_type=F32)
    g_ref[0] = _silu(gate).astype(BF16)


def _a_inproj(x, mod, pos3, consts, w_bf16):
    b, s, d = x.shape
    n = w_bf16.shape[1]
    out = jax.ShapeDtypeStruct((b, s, A_WIDTH), BF16)
    ospec = pl.BlockSpec((1, ROW_TILE, A_WIDTH), lambda bi, i: (bi, i, 0))
    return pl.pallas_call(
        _a_inproj_kernel,
        grid=(b, s // ROW_TILE),
        in_specs=[pl.BlockSpec((1, ROW_TILE, d), lambda bi, i: (bi, i, 0)),
                  pl.BlockSpec((1, 3, d), lambda bi, i: (bi, 0, 0)),
                  pl.BlockSpec((1, s // LANES, LANES), lambda bi, i: (bi, 0, 0)),
                  pl.BlockSpec((8, LANES), lambda bi, i: (0, 0)),
                  pl.BlockSpec((d, n), lambda bi, i: (0, 0))],
        out_specs=[ospec, ospec, ospec, ospec],
        out_shape=[out, out, out, out],
        compiler_params=pltpu.CompilerParams(vmem_limit_bytes=VMEM_LIMIT),
        name="a_inproj",
    )(x, mod, pos3, consts, w_bf16)


def _chunk_mask():
    krow = lax.broadcasted_iota(jnp.int32, (KV_TILE, Q_TILE), 0) // CHUNK
    qcol = lax.broadcasted_iota(jnp.int32, (KV_TILE, Q_TILE), 1) // CHUNK
    return krow <= qcol


def _score_items(q, k_rows, kv_len, s_ref, mask, out):
    diag = kv_len - KV_TILE

    def boundary():
        s_d = lax.dot_general(k_rows(diag, kv_len), q, _NT, preferred_element_type=F32)
        s_d = jnp.where(mask, s_d, NEG_INF)
        s_ref[diag:kv_len, :] = s_d
        out["m"] = jnp.max(s_d, axis=0, keepdims=True)

    def interior(r0, r1):
        s_o = lax.dot_general(k_rows(r0, r1), q, _NT, preferred_element_type=F32)
        s_ref[r0:r1, :] = s_o
        out["m"] = jnp.maximum(out["m"], jnp.max(s_o, axis=0, keepdims=True))

    items = [boundary]
    for r0 in range(0, diag, SCORE_ROWS):
        items.append(functools.partial(interior, r0, min(r0 + SCORE_ROWS, diag)))
    return items


def _exp_items(kv_len, s_ref, p_ref, state):
    def block(r0):
        p = jnp.exp2(s_ref[r0:r0 + KV_TILE, :] - state["m"])
        part = jnp.sum(p, axis=0, keepdims=True)
        state["l"] = part if "l" not in state else state["l"] + part
        p_ref[r0:r0 + KV_TILE, :] = p.astype(BF16)

    return [functools.partial(block, r0) for r0 in range(0, kv_len, KV_TILE)]


def _emit_interleaved(item_lists):
    item_lists = [items for items in item_lists if items]
    done = [0] * len(item_lists)
    for _ in range(sum(len(items) for items in item_lists)):
        i = min((i for i in range(len(item_lists)) if done[i] < len(item_lists[i])),
                key=lambda i: (done[i] + 0.5) / len(item_lists[i]))
        item_lists[i][done[i]]()
        done[i] += 1


def _attention_pipeline(n_q, n_streams, q_of, k_rows_of, vt_cols_of, s_ref, p_ref, finalize):
    mask = _chunk_mask()
    state = {}

    def slot(qi, st):
        return (qi % 2) * n_streams + st

    def pv_item(qi, st):
        kv_len = (qi + 1) * Q_TILE
        state[qi, st]["acc"] = jnp.dot(vt_cols_of(st)(0, kv_len),
                                       p_ref[slot(qi, st), 0:kv_len, :],
                                       preferred_element_type=F32)

    for t in range(n_q + 2):
        work = []
        for st in range(n_streams):
            if t < n_q:
                state[t, st] = {}
                work.append(_score_items(q_of(t, st), k_rows_of(st), (t + 1) * Q_TILE,
                                         s_ref.at[slot(t, st)], mask, state[t, st]))
            if 0 <= t - 1 < n_q:
                work.append(_exp_items(t * Q_TILE, s_ref.at[slot(t - 1, st)],
                                       p_ref.at[slot(t - 1, st)], state[t - 1, st]))
            if 0 <= t - 2 < n_q:
                work.append([functools.partial(pv_item, t - 2, st)])
        _emit_interleaved(work)
        if 0 <= t - 2 < n_q:
            finalize(t - 2, [(state[t - 2, st]["l"], state.pop((t - 2, st))["acc"])
                             for st in range(n_streams)])


def _store_transposed_values(v_ref, vt_ref):
    for r0 in range(0, v_ref.shape[1], KV_TILE):
        blk = v_ref[0, r0:r0 + KV_TILE, :].astype(F32)
        vt_ref[:, r0:r0 + KV_TILE] = blk.T.astype(BF16)


def _a_attn_kernel(lambda_init, q_ref, k_ref, v_ref, g_ref, lam_ref, subg_ref, o_ref,
                   kp_ref, vt_ref, s_ref, p_ref):
    s = q_ref.shape[1]
    lane = lax.broadcasted_iota(jnp.int32, (KV_TILE, LANES), 1)
    first_map = lane < A_HEAD_DIM
    for r0 in range(0, s, KV_TILE):
        kb = k_ref[0, r0:r0 + KV_TILE, :].astype(F32)
        kp_ref[0, r0:r0 + KV_TILE, :] = jnp.where(first_map, kb, 0.0).astype(BF16)
        kp_ref[1, r0:r0 + KV_TILE, :] = jnp.where(first_map, 0.0, kb).astype(BF16)
    _store_transposed_values(v_ref, vt_ref)

    lam = (jnp.exp(jnp.sum(lam_ref[0:1, :] * lam_ref[1:2, :], axis=1, keepdims=True))
           - jnp.exp(jnp.sum(lam_ref[2:3, :] * lam_ref[3:4, :], axis=1, keepdims=True))
           + lambda_init)
    out_gain = subg_ref[0:1, :] * (1.0 - lambda_init)

    def finalize(qi, outs):
        rows = slice(qi * Q_TILE, (qi + 1) * Q_TILE)
        (l1, acc1), (l2, acc2) = outs
        o_t = acc1 * (1.0 / l1) - acc2 * (lam / l2)
        ms = jnp.mean(o_t * o_t, axis=0, keepdims=True)
        o_t = o_t * lax.rsqrt(ms + SUBLN_EPS)
        o = o_t.T * out_gain * g_ref[0, rows, :].astype(F32)
        o_ref[0, rows, :] = o.astype(BF16)

    _attention_pipeline(
        s // Q_TILE, 2,
        lambda qi, st: q_ref[0, qi * Q_TILE:(qi + 1) * Q_TILE, :],
        lambda st: (lambda r0, r1: kp_ref[st, r0:r1, :]),
        lambda st: (lambda c0, c1: vt_ref[:, c0:c1]),
        s_ref, p_ref, finalize)


def _a_attention(q, k, v, g, lam_rows, subg_rows, lambda_init):
    b, s, _ = q.shape
    spec = pl.BlockSpec((1, s, LANES), lambda bi, h: (bi, 0, h))
    small = pl.BlockSpec((8, LANES), lambda bi, h: (0, 0))
    return pl.pallas_call(
        functools.partial(_a_attn_kernel, lambda_init),
        grid=(b, A_HEADS),
        in_specs=[spec, spec, spec, spec, small, small],
        out_specs=spec,
        out_shape=jax.ShapeDtypeStruct((b, s, A_WIDTH), BF16),
        scratch_shapes=[pltpu.VMEM((2, s, LANES), BF16),
                        pltpu.VMEM((LANES, s), BF16),
                        pltpu.VMEM((SCORE_SLOTS, s, Q_TILE), F32),
                        pltpu.VMEM((SCORE_SLOTS, s, Q_TILE), BF16)],
        compiler_params=pltpu.CompilerParams(vmem_limit_bytes=VMEM_LIMIT),
        name="a_attention",
    )(q, k, v, g, lam_rows, subg_rows)


def _outproj_ln_kernel(o_ref, w_ref, x_ref, mod_ref, lng_ref, lnb_ref, y_ref):
    y = jnp.dot(o_ref[0], w_ref[...], preferred_element_type=F32)
    gate = mod_ref[0, 2:3, :]
    z = DEEPNORM_ALPHA * x_ref[0] + gate * y
    mu = jnp.mean(z, axis=1, keepdims=True)
    zc = z - mu
    var = jnp.mean(zc * zc, axis=1, keepdims=True)
    y_ref[0] = zc * lax.rsqrt(var + LN_EPS) * lng_ref[...] + lnb_ref[...]


def _outproj_ln(o, w_bf16, x, mod, ln_g, ln_b):
    b, s, d = x.shape
    width = o.shape[2]
    row = pl.BlockSpec((1, d), lambda bi, i: (0, 0))
    return pl.pallas_call(
        _outproj_ln_kernel,
        grid=(b, s // ROW_TILE),
        in_specs=[pl.BlockSpec((1, ROW_TILE, width), lambda bi, i: (bi, i, 0)),
                  pl.BlockSpec((width, d), lambda bi, i: (0, 0)),
                  pl.BlockSpec((1, ROW_TILE, d), lambda bi, i: (bi, i, 0)),
                  pl.BlockSpec((1, 3, d), lambda bi, i: (bi, 0, 0)),
                  row, row],
        out_specs=pl.BlockSpec((1, ROW_TILE, d), lambda bi, i: (bi, i, 0)),
        out_shape=jax.ShapeDtypeStruct((b, s, d), F32),
        compiler_params=pltpu.CompilerParams(vmem_limit_bytes=VMEM_LIMIT),
        name="outproj_ln",
    )(o, w_bf16, x, mod, ln_g.reshape(1, d), ln_b.reshape(1, d))


B_HEAD_PAD = LANES
B_QK_PAD = B_HEADS * B_HEAD_PAD
B_COL_KV = B_Q_RANK
B_COL_ROPE = B_Q_RANK + B_KV_RANK
B_COL_GATE = B_COL_ROPE + LANES
B_IN_PAD = B_COL_GATE + B_WIDTH
B_COL_CHUNK = 512


def _rms(x, g_row, eps):
    ms = jnp.mean(x * x, axis=1, keepdims=True)
    return x * lax.rsqrt(ms + eps) * g_row


def _b_inproj_kernel(x_ref, mod_ref, pos_ref, consts_ref, w1_ref, qg_ref, wuq_ref, kvg_ref,
                     wk_ref, wv_ref, q_ref, k_ref, v_ref, g_ref):
    i = pl.program_id(1)
    shift = mod_ref[0, 0:1, :]
    scale = mod_ref[0, 1:2, :]
    u = (x_ref[0] * (1.0 + scale) + shift).astype(BF16)
    cos_t, sin_up, sin_dn = _rope_tables(pos_ref, i, consts_ref)
    half = B_ROPE // 2
    qk_scale = (B_NOPE + B_ROPE) ** -0.5 * LOG2E

    lat = jnp.dot(u, w1_ref[:, 0:B_COL_GATE], preferred_element_type=F32)
    qn = _rms(lat[:, 0:B_Q_RANK], qg_ref[...], RMS_EPS).astype(BF16)
    kvn = _rms(lat[:, B_COL_KV:B_COL_ROPE], kvg_ref[...], RMS_EPS).astype(BF16)
    k_rope = _rope_apply(lat[:, B_COL_ROPE:B_COL_GATE], cos_t, sin_up, sin_dn, half)

    cos_q, up_q, dn_q = cos_t * qk_scale, sin_up * qk_scale, sin_dn * qk_scale
    for c0 in range(0, B_QK_PAD, B_COL_CHUNK):
        qc = jnp.dot(qn, wuq_ref[:, c0:c0 + B_COL_CHUNK], preferred_element_type=F32)
        kc = jnp.dot(kvn, wk_ref[:, c0:c0 + B_COL_CHUNK], preferred_element_type=F32)
        for g0 in range(0, B_COL_CHUNK, B_HEAD_PAD):
            cols = slice(c0 + g0, c0 + g0 + B_HEAD_PAD)
            q_ref[0, :, cols] = _rope_apply(qc[:, g0:g0 + B_HEAD_PAD], cos_q, up_q, dn_q,
                                            half).astype(BF16)
            k_ref[0, :, cols] = (kc[:, g0:g0 + B_HEAD_PAD] + k_rope).astype(BF16)
    v_ref[0] = jnp.dot(kvn, wv_ref[...], preferred_element_type=F32).astype(BF16)
    gate = jnp.dot(u, w1_ref[:, B_COL_GATE:B_IN_PAD], preferred_element_type=F32)
    g_ref[0] = _silu(gate).astype(BF16)


def _b_inproj(x, mod, pos3, consts, w1, qg, wuq, kvg, wk, wv):
    b, s, d = x.shape

    def full(a):
        return pl.BlockSpec(a.shape, lambda bi, i: (0,) * a.ndim)

    def rows(width):
        return pl.BlockSpec((1, ROW_TILE, width), lambda bi, i: (bi, i, 0))

    return pl.pallas_call(
        _b_inproj_kernel,
        grid=(b, s // ROW_TILE),
        in_specs=[rows(d),
                  pl.BlockSpec((1, 3, d), lambda bi, i: (bi, 0, 0)),
                  pl.BlockSpec((1, s // LANES, LANES), lambda bi, i: (bi, 0, 0)),
                  full(consts), full(w1), full(qg), full(wuq), full(kvg), full(wk), full(wv)],
        out_specs=[rows(B_QK_PAD), rows(B_QK_PAD), rows(B_WIDTH), rows(B_WIDTH)],
        out_shape=[jax.ShapeDtypeStruct((b, s, B_QK_PAD), BF16),
                   jax.ShapeDtypeStruct((b, s, B_QK_PAD), BF16),
                   jax.ShapeDtypeStruct((b, s, B_WIDTH), BF16),
                   jax.ShapeDtypeStruct((b, s, B_WIDTH), BF16)],
        compiler_params=pltpu.CompilerParams(vmem_limit_bytes=VMEM_LIMIT),
        name="b_inproj",
    )(x, mod, pos3, consts, w1, qg, wuq, kvg, wk, wv)


B_HEADS_PER_STEP = LANES // B_VDIM


def _b_attn_kernel(q_ref, k_ref, v_ref, g_ref, o_ref, vt_ref, s_ref, p_ref):
    s = q_ref.shape[1]
    _store_transposed_values(v_ref, vt_ref)

    def finalize(qi, outs):
        rows = slice(qi * Q_TILE, (qi + 1) * Q_TILE)
        o_t = jnp.concatenate([acc * (1.0 / l) for l, acc in outs], axis=0)
        o_ref[0, rows, :] = (o_t.T * g_ref[0, rows, :].astype(F32)).astype(BF16)

    _attention_pipeline(
        s // Q_TILE, B_HEADS_PER_STEP,
        lambda qi, st: q_ref[0, qi * Q_TILE:(qi + 1) * Q_TILE,
                             st * B_HEAD_PAD:(st + 1) * B_HEAD_PAD],
        lambda st: (lambda r0, r1: k_ref[0, r0:r1, st * B_HEAD_PAD:(st + 1) * B_HEAD_PAD]),
        lambda st: (lambda c0, c1: vt_ref[st * B_VDIM:(st + 1) * B_VDIM, c0:c1]),
        s_ref, p_ref, finalize)


def _b_attention(q, k, v, g):
    b, s, _ = q.shape
    assert SCORE_SLOTS == 2 * B_HEADS_PER_STEP
    qk_spec = pl.BlockSpec((1, s, B_HEADS_PER_STEP * B_HEAD_PAD), lambda bi, h: (bi, 0, h))
    spec = pl.BlockSpec((1, s, LANES), lambda bi, h: (bi, 0, h))
    return pl.pallas_call(
        _b_attn_kernel,
        grid=(b, B_HEADS // B_HEADS_PER_STEP),
        in_specs=[qk_spec, qk_spec, spec, spec],
        out_specs=spec,
        out_shape=jax.ShapeDtypeStruct((b, s, B_WIDTH), BF16),
        scratch_shapes=[pltpu.VMEM((LANES, s), BF16),
                        pltpu.VMEM((SCORE_SLOTS, s, Q_TILE), F32),
                        pltpu.VMEM((SCORE_SLOTS, s, Q_TILE), BF16)],
        compiler_params=pltpu.CompilerParams(vmem_limit_bytes=VMEM_LIMIT),
        name="b_attention",
    )(q, k, v, g)


def _pad_rows8(rows):
    out = jnp.zeros((8, LANES), F32)
    for r, vec in enumerate(rows):
        out = out.at[r, :vec.shape[0]].set(vec.astype(F32))
    return out


def _b_layouts(w_in, w_uq, w_ukv):
    d = w_in.shape[0]
    rope_cols = jnp.zeros((d, LANES), F32).at[:, B_NOPE:B_NOPE + B_ROPE].set(
        w_in[:, B_COL_ROPE:B_COL_ROPE + B_ROPE])
    w1 = jnp.concatenate([w_in[:, :B_COL_ROPE], rope_cols, w_in[:, B_COL_ROPE + B_ROPE:]], axis=1)
    uq = w_uq.reshape(B_Q_RANK, B_HEADS, B_NOPE + B_ROPE)
    uq = jnp.pad(uq, ((0, 0), (0, 0), (0, B_HEAD_PAD - B_NOPE - B_ROPE)))
    ukv = w_ukv.reshape(B_KV_RANK, B_HEADS, B_NOPE + B_VDIM)
    wk = jnp.pad(ukv[:, :, :B_NOPE], ((0, 0), (0, 0), (0, B_HEAD_PAD - B_NOPE)))
    wv = ukv[:, :, B_NOPE:]
    return (w1.astype(BF16), uq.reshape(B_Q_RANK, B_QK_PAD).astype(BF16),
            wk.reshape(B_KV_RANK, B_QK_PAD).astype(BF16),
            wv.reshape(B_KV_RANK, B_WIDTH).astype(BF16))


def kernel(x, c, positions, ada_w, ada_b, ln_g, ln_b, a_w_in, a_lambda_q1, a_lambda_k1,
           a_lambda_q2, a_lambda_k2, a_subln_g, a_w_out, b_w_in, b_q_norm_g, b_w_uq,
           b_kv_norm_g, b_w_ukv, b_w_out):
    b, s, d = x.shape
    assert d == D_MODEL and s % ROW_TILE == 0 and s % Q_TILE == 0 and Q_TILE == KV_TILE
    mod = _modulation(c, ada_w, ada_b).reshape(DEPTH, b, 3, d)
    pos3 = positions.reshape(b, s // LANES, LANES)

    lambda_init = 0.8 - 0.6 * math.exp(-0.3 * 0)
    q, k, v, g = _a_inproj(x, mod[0], pos3, _rope_consts(A_ROT, A_HEAD_DIM, 0),
                           a_w_in[0].astype(BF16))
    lam_rows = _pad_rows8([a_lambda_q1[0], a_lambda_k1[0], a_lambda_q2[0], a_lambda_k2[0]])
    o = _a_attention(q, k, v, g, lam_rows, _pad_rows8([a_subln_g[0]]), lambda_init)
    x = _outproj_ln(o, a_w_out[0].astype(BF16), x, mod[0], ln_g[0], ln_b[0])

    w1, wuq, wk, wv = _b_layouts(b_w_in[0], b_w_uq[0], b_w_ukv[0])
    q, k, v, g = _b_inproj(x, mod[1], pos3, _rope_consts(B_ROPE, B_HEAD_PAD, B_NOPE), w1,
                           b_q_norm_g[0].reshape(1, B_Q_RANK), wuq,
                           b_kv_norm_g[0].reshape(1, B_KV_RANK), wk, wv)
    o = _b_attention(q, k, v, g)
    return _outproj_ln(o, b_w_out[0].astype(BF16), x, mod[1], ln_g[1], ln_b[1])
```

```python
import functools
import math

import jax
import jax.numpy as jnp
import numpy as np
from jax import lax
from jax.experimental import pallas as pl
from jax.experimental.pallas import tpu as pltpu

F32 = jnp.float32
BF16 = jnp.bfloat16

D_MODEL = 1024
CHUNK = 64
ROPE_THETA = 500000.0

A_HEADS = 8
A_HEAD_DIM = 64
A_ROT = A_HEAD_DIM // 4
A_WIDTH = A_HEADS * 2 * A_HEAD_DIM

B_HEADS = 16
B_NOPE = 64
B_ROPE = 32
B_VDIM = 64
B_Q_RANK = 512
B_KV_RANK = 256
B_WIDTH = B_HEADS * B_VDIM

DEPTH = 2
DEEPNORM_ALPHA = (2.0 * DEPTH) ** 0.25
LN_EPS = 1e-5
RMS_EPS = 1e-6
SUBLN_EPS = 1e-5
NEG_INF = -1e30

LANES = 128
ROW_TILE = 512
Q_TILE = 256
KV_TILE = 256
SCORE_ROWS = 512
SCORE_SLOTS = 4
ONES_ROWS = 16
LOG2E = math.log2(math.e)
VMEM_LIMIT = 56 * 1024 * 1024

_NT = (((1,), (1,)), ((), ()))


def _silu(x):
    return x * (1.0 / (1.0 + jnp.exp(-x)))


def _mod_kernel(c_ref, w_ref, b_ref, o_ref):
    ca = _silu(c_ref[...])
    o_ref[0] = jnp.dot(ca, w_ref[0], precision=lax.Precision.HIGHEST,
                       preferred_element_type=F32) + b_ref[0]


def _modulation(c, ada_w, ada_b):
    depth, d, n3 = ada_w.shape
    b = c.shape[0]
    tn = 1024
    return pl.pallas_call(
        _mod_kernel,
        grid=(depth, n3 // tn),
        in_specs=[pl.BlockSpec((b, d), lambda i, j: (0, 0)),
                  pl.BlockSpec((1, d, tn), lambda i, j: (i, 0, j)),
                  pl.BlockSpec((1, 1, tn), lambda i, j: (i, 0, j))],
        out_specs=pl.BlockSpec((1, b, tn), lambda i, j: (i, 0, j)),
        out_shape=jax.ShapeDtypeStruct((depth, b, n3), F32),
        compiler_params=pltpu.CompilerParams(vmem_limit_bytes=VMEM_LIMIT),
        name="modulation",
    )(c, ada_w, ada_b.reshape(depth, 1, n3))


def _rope_tables(pos_ref, row_tile_idx, consts_ref):
    invf = consts_ref[0:1, :]
    cs, sn = [], []
    for r in range(ROW_TILE // LANES):
        row = pos_ref[0, pl.ds(row_tile_idx * (ROW_TILE // LANES) + r, 1), :].astype(F32)
        pos_col = jnp.broadcast_to(row, (LANES, LANES)).T
        ang = pos_col * invf
        cs.append(jnp.cos(ang))
        sn.append(jnp.sin(ang))
    cos_t = jnp.concatenate(cs, axis=0)
    sin_t = jnp.concatenate(sn, axis=0)
    return cos_t, sin_t * consts_ref[1:2, :], sin_t * consts_ref[2:3, :]


def _rope_apply(a, cos_t, sin_up, sin_dn, half):
    return (a * cos_t + pltpu.roll(a, LANES - half, 1) * sin_up
            + pltpu.roll(a, half, 1) * sin_dn)


def _rope_consts(rot_dim, group, offset):
    inv_freq = ROPE_THETA ** (-jnp.arange(0, rot_dim, 2, dtype=F32) / rot_dim)
    half = rot_dim // 2
    lane = np.arange(LANES)
    d = lane % group - offset
    rotated = (d >= 0) & (d < rot_dim)
    first = rotated & (d < half)
    second = rotated & (d >= half)
    idx = np.where(rotated, d % half, 0)
    invf = jnp.where(jnp.asarray(rotated), inv_freq[idx], 0.0)
    rows = jnp.stack([invf, jnp.asarray(-first.astype(np.float32)),
                      jnp.asarray(second.astype(np.float32))])
    return jnp.concatenate([rows, jnp.zeros((5, LANES), F32)], axis=0)


def _a_inproj_kernel(x_ref, mod_ref, pos_ref, consts_ref, w_ref, q_ref, k_ref, v_ref, g_ref):
    i = pl.program_id(1)
    shift = mod_ref[0, 0:1, :]
    scale = mod_ref[0, 1:2, :]
    u = (x_ref[0] * (1.0 + scale) + shift).astype(BF16)
    cos_t, sin_up, sin_dn = _rope_tables(pos_ref, i, consts_ref)
    half = A_ROT // 2
    qk_scale = A_HEAD_DIM ** -0.5 * LOG2E

    def rope_store(col0, o_ref, mult):
        acc = jnp.dot(u, w_ref[:, col0:col0 + A_WIDTH], preferred_element_type=F32)
        for g in range(A_WIDTH // LANES):
            a = acc[:, g * LANES:(g + 1) * LANES]
            r = _rope_apply(a, cos_t, sin_up, sin_dn, half)
            if mult != 1.0:
                r = r * mult
            o_ref[0, :, g * LANES:(g + 1) * LANES] = r.astype(BF16)

    rope_store(0, q_ref, qk_scale)
    rope_store(A_WIDTH, k_ref, 1.0)
    v_ref[0] = jnp.dot(u, w_ref[:, 2 * A_WIDTH:3 * A_WIDTH],
                       preferred_element_type=F32).astype(BF16)
    gate = jnp.dot(u, w_ref[:, 3 * A_WIDTH:4 * A_WIDTH], preferred_element_type=F32)
    g_ref[0] = _silu(gate).astype(BF16)


def _a_inproj(x, mod, pos3, consts, w_bf16):
    b, s, d = x.shape
    n = w_bf16.shape[1]
    out = jax.ShapeDtypeStruct((b, s, A_WIDTH), BF16)
    ospec = pl.BlockSpec((1, ROW_TILE, A_WIDTH), lambda bi, i: (bi, i, 0))
    return pl.pallas_call(
        _a_inproj_kernel,
        grid=(b, s // ROW_TILE),
        in_specs=[pl.BlockSpec((1, ROW_TILE, d), lambda bi, i: (bi, i, 0)),
                  pl.BlockSpec((1, 3, d), lambda bi, i: (bi, 0, 0)),
                  pl.BlockSpec((1, s // LANES, LANES), lambda bi, i: (bi, 0, 0)),
                  pl.BlockSpec((8, LANES), lambda bi, i: (0, 0)),
                  pl.BlockSpec((d, n), lambda bi, i: (0, 0))],
        out_specs=[ospec, ospec, ospec, ospec],
        out_shape=[out, out, out, out],
        compiler_params=pltpu.CompilerParams(vmem_limit_bytes=VMEM_LIMIT),
        name="a_inproj",
    )(x, mod, pos3, consts, w_bf16)


def _chunk_mask():
    krow = lax.broadcasted_iota(jnp.int32, (KV_TILE, Q_TILE), 0) // CHUNK
    qcol = lax.broadcasted_iota(jnp.int32, (KV_TILE, Q_TILE), 1) // CHUNK
    return krow <= qcol


def _score_items(q, k_rows, kv_len, s_ref, mask, out):
    diag = kv_len - KV_TILE

    def boundary():
        s_d = lax.dot_general(k_rows(diag, kv_len), q, _NT, preferred_element_type=F32)
        s_d = jnp.where(mask, s_d, NEG_INF).astype(BF16)
        s_ref[diag:kv_len, :] = s_d
        out["m"] = jnp.max(s_d, axis=0, keepdims=True)

    def interior(r0, r1):
        s_o = lax.dot_general(k_rows(r0, r1), q, _NT,
                              preferred_element_type=F32).astype(BF16)
        s_ref[r0:r1, :] = s_o
        out["m"] = jnp.maximum(out["m"], jnp.max(s_o, axis=0, keepdims=True))

    items = [boundary]
    for r0 in range(0, diag, SCORE_ROWS):
        items.append(functools.partial(interior, r0, min(r0 + SCORE_ROWS, diag)))
    return items


def _exp_items(kv_len, s_ref, p_ref, state):
    def block(r0):
        p_ref[r0:r0 + KV_TILE, :] = jnp.exp2(s_ref[r0:r0 + KV_TILE, :] - state["m"])

    return [functools.partial(block, r0) for r0 in range(0, kv_len, KV_TILE)]


def _emit_interleaved(item_lists):
    item_lists = [items for items in item_lists if items]
    done = [0] * len(item_lists)
    for _ in range(sum(len(items) for items in item_lists)):
        i = min((i for i in range(len(item_lists)) if done[i] < len(item_lists[i])),
                key=lambda i: (done[i] + 0.5) / len(item_lists[i]))
        item_lists[i][done[i]]()
        done[i] += 1


def _attention_pipeline(n_q, n_streams, dv, q_of, k_rows_of, vt_cols_of, s_ref, p_ref,
                        finalize):
    mask = _chunk_mask()
    state = {}

    def slot(qi, st):
        return (qi % 2) * n_streams + st

    def pv_item(qi, st):
        kv_len = (qi + 1) * Q_TILE
        acc = jnp.dot(vt_cols_of(st)(0, kv_len), p_ref[slot(qi, st), 0:kv_len, :],
                      preferred_element_type=F32)
        state[qi, st]["acc"] = acc[0:dv, :]
        state[qi, st]["l"] = acc[dv:dv + 1, :]

    for t in range(n_q + 2):
        work = []
        for st in range(n_streams):
            if t < n_q:
                state[t, st] = {}
                work.append(_score_items(q_of(t, st), k_rows_of(st), (t + 1) * Q_TILE,
                                         s_ref.at[slot(t, st)], mask, state[t, st]))
            if 0 <= t - 1 < n_q:
                work.append(_exp_items(t * Q_TILE, s_ref.at[slot(t - 1, st)],
                                       p_ref.at[slot(t - 1, st)], state[t - 1, st]))
            if 0 <= t - 2 < n_q:
                work.append([functools.partial(pv_item, t - 2, st)])
        _emit_interleaved(work)
        if 0 <= t - 2 < n_q:
            finalize(t - 2, [(state[t - 2, st]["l"], state.pop((t - 2, st))["acc"])
                             for st in range(n_streams)])


def _store_transposed_values(v_ref, vt_ref, dv):
    n_heads = v_ref.shape[2] // dv
    stride = dv + ONES_ROWS
    ones = jnp.ones((ONES_ROWS, KV_TILE), BF16)
    for r0 in range(0, v_ref.shape[1], KV_TILE):
        blk_t = v_ref[0, r0:r0 + KV_TILE, :].astype(F32).T.astype(BF16)
        for h in range(n_heads):
            vt_ref[h * stride:h * stride + dv, r0:r0 + KV_TILE] = blk_t[h * dv:(h + 1) * dv, :]
            vt_ref[h * stride + dv:(h + 1) * stride, r0:r0 + KV_TILE] = ones


def _a_attn_kernel(lambda_init, q_ref, k_ref, v_ref, g_ref, lam_ref, subg_ref, o_ref,
                   kp_ref, vt_ref, s_ref, p_ref):
    s = q_ref.shape[1]
    lane = lax.broadcasted_iota(jnp.int32, (KV_TILE, LANES), 1)
    first_map = lane < A_HEAD_DIM
    for r0 in range(0, s, KV_TILE):
        kb = k_ref[0, r0:r0 + KV_TILE, :].astype(F32)
        kp_ref[0, r0:r0 + KV_TILE, :] = jnp.where(first_map, kb, 0.0).astype(BF16)
        kp_ref[1, r0:r0 + KV_TILE, :] = jnp.where(first_map, 0.0, kb).astype(BF16)
    dv = 2 * A_HEAD_DIM
    _store_transposed_values(v_ref, vt_ref, dv)

    lam = (jnp.exp(jnp.sum(lam_ref[0:1, :] * lam_ref[1:2, :], axis=1, keepdims=True))
           - jnp.exp(jnp.sum(lam_ref[2:3, :] * lam_ref[3:4, :], axis=1, keepdims=True))
           + lambda_init)
    out_gain = subg_ref[0:1, :] * (1.0 - lambda_init)

    def finalize(qi, outs):
        rows = slice(qi * Q_TILE, (qi + 1) * Q_TILE)
        (l1, acc1), (l2, acc2) = outs
        o_t = acc1 * (1.0 / l1) - acc2 * (lam / l2)
        ms = jnp.mean(o_t * o_t, axis=0, keepdims=True)
        o_t = o_t * lax.rsqrt(ms + SUBLN_EPS)
        o = o_t.T * out_gain * g_ref[0, rows, :].astype(F32)
        o_ref[0, rows, :] = o.astype(BF16)

    _attention_pipeline(
        s // Q_TILE, 2, dv,
        lambda qi, st: q_ref[0, qi * Q_TILE:(qi + 1) * Q_TILE, :],
        lambda st: (lambda r0, r1: kp_ref[st, r0:r1, :]),
        lambda st: (lambda c0, c1: vt_ref[:, c0:c1]),
        s_ref, p_ref, finalize)


def _a_attention(q, k, v, g, lam_rows, subg_rows, lambda_init):
    b, s, _ = q.shape
    spec = pl.BlockSpec((1, s, LANES), lambda bi, h: (bi, 0, h))
    small = pl.BlockSpec((8, LANES), lambda bi, h: (0, 0))
    return pl.pallas_call(
        functools.partial(_a_attn_kernel, lambda_init),
        grid=(b, A_HEADS),
        in_specs=[spec, spec, spec, spec, small, small],
        out_specs=spec,
        out_shape=jax.ShapeDtypeStruct((b, s, A_WIDTH), BF16),
        scratch_shapes=[pltpu.VMEM((2, s, LANES), BF16),
                        pltpu.VMEM((LANES + ONES_ROWS, s), BF16),
                        pltpu.VMEM((SCORE_SLOTS, s, Q_TILE), BF16),
                        pltpu.VMEM((SCORE_SLOTS, s, Q_TILE), BF16)],
        compiler_params=pltpu.CompilerParams(vmem_limit_bytes=VMEM_LIMIT),
        name="a_attention",
    )(q, k, v, g, lam_rows, subg_rows)


def _outproj_ln_kernel(o_ref, w_ref, x_ref, mod_ref, lng_ref, lnb_ref, y_ref):
    y = jnp.dot(o_ref[0], w_ref[...], preferred_element_type=F32)
    gate = mod_ref[0, 2:3, :]
    z = DEEPNORM_ALPHA * x_ref[0] + gate * y
    mu = jnp.mean(z, axis=1, keepdims=True)
    zc = z - mu
    var = jnp.mean(zc * zc, axis=1, keepdims=True)
    y_ref[0] = zc * lax.rsqrt(var + LN_EPS) * lng_ref[...] + lnb_ref[...]


def _outproj_ln(o, w_bf16, x, mod, ln_g, ln_b):
    b, s, d = x.shape
    width = o.shape[2]
    row = pl.BlockSpec((1, d), lambda bi, i: (0, 0))
    return pl.pallas_call(
        _outproj_ln_kernel,
        grid=(b, s // ROW_TILE),
        in_specs=[pl.BlockSpec((1, ROW_TILE, width), lambda bi, i: (bi, i, 0)),
                  pl.BlockSpec((width, d), lambda bi, i: (0, 0)),
                  pl.BlockSpec((1, ROW_TILE, d), lambda bi, i: (bi, i, 0)),
                  pl.BlockSpec((1, 3, d), lambda bi, i: (bi, 0, 0)),
                  row, row],
        out_specs=pl.BlockSpec((1, ROW_TILE, d), lambda bi, i: (bi, i, 0)),
        out_shape=jax.ShapeDtypeStruct((b, s, d), F32),
        compiler_params=pltpu.CompilerParams(vmem_limit_bytes=VMEM_LIMIT),
        name="outproj_ln",
    )(o, w_bf16, x, mod, ln_g.reshape(1, d), ln_b.reshape(1, d))


B_HEAD_PAD = LANES
B_QK_PAD = B_HEADS * B_HEAD_PAD
B_COL_KV = B_Q_RANK
B_COL_ROPE = B_Q_RANK + B_KV_RANK
B_COL_GATE = B_COL_ROPE + LANES
B_IN_PAD = B_COL_GATE + B_WIDTH
B_COL_CHUNK = 512


def _rms(x, g_row, eps):
    ms = jnp.mean(x * x, axis=1, keepdims=True)
    return x * lax.rsqrt(ms + eps) * g_row


def _b_inproj_kernel(x_ref, mod_ref, pos_ref, consts_ref, w1_ref, qg_ref, wuq_ref, kvg_ref,
                     wk_ref, wv_ref, q_ref, k_ref, v_ref, g_ref):
    i = pl.program_id(1)
    shift = mod_ref[0, 0:1, :]
    scale = mod_ref[0, 1:2, :]
    u = (x_ref[0] * (1.0 + scale) + shift).astype(BF16)
    cos_t, sin_up, sin_dn = _rope_tables(pos_ref, i, consts_ref)
    half = B_ROPE // 2
    qk_scale = (B_NOPE + B_ROPE) ** -0.5 * LOG2E

    lat = jnp.dot(u, w1_ref[:, 0:B_COL_GATE], preferred_element_type=F32)
    qn = _rms(lat[:, 0:B_Q_RANK], qg_ref[...], RMS_EPS).astype(BF16)
    kvn = _rms(lat[:, B_COL_KV:B_COL_ROPE], kvg_ref[...], RMS_EPS).astype(BF16)
    k_rope = _rope_apply(lat[:, B_COL_ROPE:B_COL_GATE], cos_t, sin_up, sin_dn, half)

    cos_q, up_q, dn_q = cos_t * qk_scale, sin_up * qk_scale, sin_dn * qk_scale
    for c0 in range(0, B_QK_PAD, B_COL_CHUNK):
        qc = jnp.dot(qn, wuq_ref[:, c0:c0 + B_COL_CHUNK], preferred_element_type=F32)
        kc = jnp.dot(kvn, wk_ref[:, c0:c0 + B_COL_CHUNK], preferred_element_type=F32)
        for g0 in range(0, B_COL_CHUNK, B_HEAD_PAD):
            cols = slice(c0 + g0, c0 + g0 + B_HEAD_PAD)
            q_ref[0, :, cols] = _rope_apply(qc[:, g0:g0 + B_HEAD_PAD], cos_q, up_q, dn_q,
                                            half).astype(BF16)
            k_ref[0, :, cols] = (kc[:, g0:g0 + B_HEAD_PAD] + k_rope).astype(BF16)
    v_ref[0] = jnp.dot(kvn, wv_ref[...], preferred_element_type=F32).astype(BF16)
    gate = jnp.dot(u, w1_ref[:, B_COL_GATE:B_IN_PAD], preferred_element_type=F32)
    g_ref[0] = _silu(gate).astype(BF16)


def _b_inproj(x, mod, pos3, consts, w1, qg, wuq, kvg, wk, wv):
    b, s, d = x.shape

    def full(a):
        return pl.BlockSpec(a.shape, lambda bi, i: (0,) * a.ndim)

    def rows(width):
        return pl.BlockSpec((1, ROW_TILE, width), lambda bi, i: (bi, i, 0))

    return pl.pallas_call(
        _b_inproj_kernel,
        grid=(b, s // ROW_TILE),
        in_specs=[rows(d),
                  pl.BlockSpec((1, 3, d), lambda bi, i: (bi, 0, 0)),
                  pl.BlockSpec((1, s // LANES, LANES), lambda bi, i: (bi, 0, 0)),
                  full(consts), full(w1), full(qg), full(wuq), full(kvg), full(wk), full(wv)],
        out_specs=[rows(B_QK_PAD), rows(B_QK_PAD), rows(B_WIDTH), rows(B_WIDTH)],
        out_shape=[jax.ShapeDtypeStruct((b, s, B_QK_PAD), BF16),
                   jax.ShapeDtypeStruct((b, s, B_QK_PAD), BF16),
                   jax.ShapeDtypeStruct((b, s, B_WIDTH), BF16),
                   jax.ShapeDtypeStruct((b, s, B_WIDTH), BF16)],
        compiler_params=pltpu.CompilerParams(vmem_limit_bytes=VMEM_LIMIT),
        name="b_inproj",
    )(x, mod, pos3, consts, w1, qg, wuq, kvg, wk, wv)


B_HEADS_PER_STEP = LANES // B_VDIM


def _b_attn_kernel(q_ref, k_ref, v_ref, g_ref, o_ref, vt_ref, s_ref, p_ref):
    s = q_ref.shape[1]
    _store_transposed_values(v_ref, vt_ref, B_VDIM)
    stride = B_VDIM + ONES_ROWS

    def finalize(qi, outs):
        rows = slice(qi * Q_TILE, (qi + 1) * Q_TILE)
        o_t = jnp.concatenate([acc * (1.0 / l) for l, acc in outs], axis=0)
        o_ref[0, rows, :] = (o_t.T * g_ref[0, rows, :].astype(F32)).astype(BF16)

    _attention_pipeline(
        s // Q_TILE, B_HEADS_PER_STEP, B_VDIM,
        lambda qi, st: q_ref[0, qi * Q_TILE:(qi + 1) * Q_TILE,
                             st * B_HEAD_PAD:(st + 1) * B_HEAD_PAD],
        lambda st: (lambda r0, r1: k_ref[0, r0:r1, st * B_HEAD_PAD:(st + 1) * B_HEAD_PAD]),
        lambda st: (lambda c0, c1: vt_ref[st * stride:(st + 1) * stride, c0:c1]),
        s_ref, p_ref, finalize)


def _b_attention(q, k, v, g):
    b, s, _ = q.shape
    assert SCORE_SLOTS == 2 * B_HEADS_PER_STEP
    qk_spec = pl.BlockSpec((1, s, B_HEADS_PER_STEP * B_HEAD_PAD), lambda bi, h: (bi, 0, h))
    spec = pl.BlockSpec((1, s, LANES), lambda bi, h: (bi, 0, h))
    return pl.pallas_call(
        _b_attn_kernel,
        grid=(b, B_HEADS // B_HEADS_PER_STEP),
        in_specs=[qk_spec, qk_spec, spec, spec],
        out_specs=spec,
        out_shape=jax.ShapeDtypeStruct((b, s, B_WIDTH), BF16),
        scratch_shapes=[pltpu.VMEM((B_HEADS_PER_STEP * (B_VDIM + ONES_ROWS), s), BF16),
                        pltpu.VMEM((SCORE_SLOTS, s, Q_TILE), BF16),
                        pltpu.VMEM((SCORE_SLOTS, s, Q_TILE), BF16)],
        compiler_params=pltpu.CompilerParams(vmem_limit_bytes=VMEM_LIMIT),
        name="b_attention",
    )(q, k, v, g)


def _pad_rows8(rows):
    out = jnp.zeros((8, LANES), F32)
    for r, vec in enumerate(rows):
        out = out.at[r, :vec.shape[0]].set(vec.astype(F32))
    return out


def _b_layouts(w_in, w_uq, w_ukv):
    d = w_in.shape[0]
    rope_cols = jnp.zeros((d, LANES), F32).at[:, B_NOPE:B_NOPE + B_ROPE].set(
        w_in[:, B_COL_ROPE:B_COL_ROPE + B_ROPE])
    w1 = jnp.concatenate([w_in[:, :B_COL_ROPE], rope_cols, w_in[:, B_COL_ROPE + B_ROPE:]], axis=1)
    uq = w_uq.reshape(B_Q_RANK, B_HEADS, B_NOPE + B_ROPE)
    uq = jnp.pad(uq, ((0, 0), (0, 0), (0, B_HEAD_PAD - B_NOPE - B_ROPE)))
    ukv = w_ukv.reshape(B_KV_RANK, B_HEADS, B_NOPE + B_VDIM)
    wk = jnp.pad(ukv[:, :, :B_NOPE], ((0, 0), (0, 0), (0, B_HEAD_PAD - B_NOPE)))
    wv = ukv[:, :, B_NOPE:]
    return (w1.astype(BF16), uq.reshape(B_Q_RANK, B_QK_PAD).astype(BF16),
            wk.reshape(B_KV_RANK, B_QK_PAD).astype(BF16),
            wv.reshape(B_KV_RANK, B_WIDTH).astype(BF16))


def kernel(x, c, positions, ada_w, ada_b, ln_g, ln_b, a_w_in, a_lambda_q1, a_lambda_k1,
           a_lambda_q2, a_lambda_k2, a_subln_g, a_w_out, b_w_in, b_q_norm_g, b_w_uq,
           b_kv_norm_g, b_w_ukv, b_w_out):
    b, s, d = x.shape
    assert d == D_MODEL and s % ROW_TILE == 0 and s % Q_TILE == 0 and Q_TILE == KV_TILE
    mod = _modulation(c, ada_w, ada_b).reshape(DEPTH, b, 3, d)
    pos3 = positions.reshape(b, s // LANES, LANES)

    lambda_init = 0.8 - 0.6 * math.exp(-0.3 * 0)
    q, k, v, g = _a_inproj(x, mod[0], pos3, _rope_consts(A_ROT, A_HEAD_DIM, 0),
                           a_w_in[0].astype(BF16))
    lam_rows = _pad_rows8([a_lambda_q1[0], a_lambda_k1[0], a_lambda_q2[0], a_lambda_k2[0]])
    o = _a_attention(q, k, v, g, lam_rows, _pad_rows8([a_subln_g[0]]), lambda_init)
    x = _outproj_ln(o, a_w_out[0].astype(BF16), x, mod[0], ln_g[0], ln_b[0])

    w1, wuq, wk, wv = _b_layouts(b_w_in[0], b_w_uq[0], b_w_ukv[0])
    q, k, v, g = _b_inproj(x, mod[1], pos3, _rope_consts(B_ROPE, B_HEAD_PAD, B_NOPE), w1,
                           b_q_norm_g[0].reshape(1, B_Q_RANK), wuq,
                           b_kv_norm_g[0].reshape(1, B_KV_RANK), wk, wv)
    o = _b_attention(q, k, v, g)
    return _outproj_ln(o, b_w_out[0].astype(BF16), x, mod[1], ln_g[1], ln_b[1])
```

```python
import functools
import math

import jax
import jax.numpy as jnp
import numpy as np
from jax import lax
from jax.experimental import pallas as pl
from jax.experimental.pallas import tpu as pltpu

F32 = jnp.float32
BF16 = jnp.bfloat16

D_MODEL = 1024
CHUNK = 64
ROPE_THETA = 500000.0

A_HEADS = 8
A_HEAD_DIM = 64
A_ROT = A_HEAD_DIM // 4
A_WIDTH = A_HEADS * 2 * A_HEAD_DIM

B_HEADS = 16
B_NOPE = 64
B_ROPE = 32
B_VDIM = 64
B_Q_RANK = 512
B_KV_RANK = 256
B_WIDTH = B_HEADS * B_VDIM

DEPTH = 2
DEEPNORM_ALPHA = (2.0 * DEPTH) ** 0.25
LN_EPS = 1e-5
RMS_EPS = 1e-6
SUBLN_EPS = 1e-5
NEG_INF = -1e30

LANES = 128
ROW_TILE = 512
Q_TILE = 256
KV_TILE = 256
SCORE_ROWS = 512
SCORE_SLOTS = 4
ONES_ROWS = 16
LOG2E = math.log2(math.e)
VMEM_LIMIT = 56 * 1024 * 1024

_NT = (((1,), (1,)), ((), ()))


def _silu(x):
    return x * (1.0 / (1.0 + jnp.exp(-x)))


def _mod_kernel(c_ref, w_ref, b_ref, o_ref):
    ca = _silu(c_ref[...])
    o_ref[0] = jnp.dot(ca, w_ref[0], precision=lax.Precision.HIGHEST,
                       preferred_element_type=F32) + b_ref[0]


def _modulation(c, ada_w, ada_b):
    depth, d, n3 = ada_w.shape
    b = c.shape[0]
    tn = 1024
    return pl.pallas_call(
        _mod_kernel,
        grid=(depth, n3 // tn),
        in_specs=[pl.BlockSpec((b, d), lambda i, j: (0, 0)),
                  pl.BlockSpec((1, d, tn), lambda i, j: (i, 0, j)),
                  pl.BlockSpec((1, 1, tn), lambda i, j: (i, 0, j))],
        out_specs=pl.BlockSpec((1, b, tn), lambda i, j: (i, 0, j)),
        out_shape=jax.ShapeDtypeStruct((depth, b, n3), F32),
        compiler_params=pltpu.CompilerParams(vmem_limit_bytes=VMEM_LIMIT),
        name="modulation",
    )(c, ada_w, ada_b.reshape(depth, 1, n3))


def _rope_tables(pos_ref, row_tile_idx, consts_ref):
    invf = consts_ref[0:1, :]
    cs, sn = [], []
    for r in range(ROW_TILE // LANES):
        row = pos_ref[0, pl.ds(row_tile_idx * (ROW_TILE // LANES) + r, 1), :].astype(F32)
        pos_col = jnp.broadcast_to(row, (LANES, LANES)).T
        ang = pos_col * invf
        cs.append(jnp.cos(ang))
        sn.append(jnp.sin(ang))
    cos_t = jnp.concatenate(cs, axis=0)
    sin_t = jnp.concatenate(sn, axis=0)
    return cos_t, sin_t * consts_ref[1:2, :], sin_t * consts_ref[2:3, :]


def _rope_apply(a, cos_t, sin_up, sin_dn, half):
    return (a * cos_t + pltpu.roll(a, LANES - half, 1) * sin_up
            + pltpu.roll(a, half, 1) * sin_dn)


def _rope_consts(rot_dim, group, offset):
    inv_freq = ROPE_THETA ** (-jnp.arange(0, rot_dim, 2, dtype=F32) / rot_dim)
    half = rot_dim // 2
    lane = np.arange(LANES)
    d = lane % group - offset
    rotated = (d >= 0) & (d < rot_dim)
    first = rotated & (d < half)
    second = rotated & (d >= half)
    idx = np.where(rotated, d % half, 0)
    invf = jnp.where(jnp.asarray(rotated), inv_freq[idx], 0.0)
    rows = jnp.stack([invf, jnp.asarray(-first.astype(np.float32)),
                      jnp.asarray(second.astype(np.float32))])
    return jnp.concatenate([rows, jnp.zeros((5, LANES), F32)], axis=0)


def _a_inproj_kernel(x_ref, mod_ref, pos_ref, consts_ref, w_ref, q_ref, k_ref, v_ref, g_ref):
    i = pl.program_id(1)
    shift = mod_ref[0, 0:1, :]
    scale = mod_ref[0, 1:2, :]
    u = (x_ref[0] * (1.0 + scale) + shift).astype(BF16)
    cos_t, sin_up, sin_dn = _rope_tables(pos_ref, i, consts_ref)
    half = A_ROT // 2
    qk_scale = A_HEAD_DIM ** -0.5 * LOG2E

    def rope_store(col0, o_ref, mult):
        acc = jnp.dot(u, w_ref[:, col0:col0 + A_WIDTH], preferred_element_type=F32)
        for g in range(A_WIDTH // LANES):
            a = acc[:, g * LANES:(g + 1) * LANES]
            r = _rope_apply(a, cos_t, sin_up, sin_dn, half)
            if mult != 1.0:
                r = r * mult
            o_ref[0, :, g * LANES:(g + 1) * LANES] = r.astype(BF16)

    rope_store(0, q_ref, qk_scale)
    rope_store(A_WIDTH, k_ref, 1.0)
    v_ref[0] = jnp.dot(u, w_ref[:, 2 * A_WIDTH:3 * A_WIDTH],
                       preferred_element_type=F32).astype(BF16)
    gate = jnp.dot(u, w_ref[:, 3 * A_WIDTH:4 * A_WIDTH], preferred_element_type=F32)
    g_ref[0] = _silu(gate).astype(BF16)


def _a_inproj(x, mod, pos3, consts, w_bf16):
    b, s, d = x.shape
    n = w_bf16.shape[1]
    out = jax.ShapeDtypeStruct((b, s, A_WIDTH), BF16)
    ospec = pl.BlockSpec((1, ROW_TILE, A_WIDTH), lambda bi, i: (bi, i, 0))
    return pl.pallas_call(
        _a_inproj_kernel,
        grid=(b, s // ROW_TILE),
        in_specs=[pl.BlockSpec((1, ROW_TILE, d), lambda bi, i: (bi, i, 0)),
                  pl.BlockSpec((1, 3, d), lambda bi, i: (bi, 0, 0)),
                  pl.BlockSpec((1, s // LANES, LANES), lambda bi, i: (bi, 0, 0)),
                  pl.BlockSpec((8, LANES), lambda bi, i: (0, 0)),
                  pl.BlockSpec((d, n), lambda bi, i: (0, 0))],
        out_specs=[ospec, ospec, ospec, ospec],
        out_shape=[out, out, out, out],
        compiler_params=pltpu.CompilerParams(vmem_limit_bytes=VMEM_LIMIT),
        name="a_inproj",
    )(x, mod, pos3, consts, w_bf16)


def _chunk_mask():
    krow = lax.broadcasted_iota(jnp.int32, (KV_TILE, Q_TILE), 0) // CHUNK
    qcol = lax.broadcasted_iota(jnp.int32, (KV_TILE, Q_TILE), 1) // CHUNK
    return krow <= qcol


def _score_items(q, k_rows, kv_len, s_ref, mask, out):
    diag = kv_len - KV_TILE

    def boundary():
        s_d = lax.dot_general(k_rows(diag, kv_len), q, _NT, preferred_element_type=F32)
        s_d = jnp.where(mask, s_d, NEG_INF).astype(BF16)
        s_ref[diag:kv_len, :] = s_d
        out["m"] = jnp.max(s_d, axis=0, keepdims=True)

    def interior(r0, r1):
        s_o = lax.dot_general(k_rows(r0, r1), q, _NT,
                              preferred_element_type=F32).astype(BF16)
        s_ref[r0:r1, :] = s_o
        out["m"] = jnp.maximum(out["m"], jnp.max(s_o, axis=0, keepdims=True))

    items = [boundary]
    for r0 in range(0, diag, SCORE_ROWS):
        items.append(functools.partial(interior, r0, min(r0 + SCORE_ROWS, diag)))
    return items


def _exp_items(kv_len, s_ref, p_ref, state):
    def block(r0):
        d = (s_ref[r0:r0 + KV_TILE, :] - state["m"]).astype(F32)
        p_ref[r0:r0 + KV_TILE, :] = jnp.exp2(d).astype(BF16)

    return [functools.partial(block, r0) for r0 in range(0, kv_len, KV_TILE)]


def _emit_interleaved(item_lists):
    item_lists = [items for items in item_lists if items]
    done = [0] * len(item_lists)
    for _ in range(sum(len(items) for items in item_lists)):
        i = min((i for i in range(len(item_lists)) if done[i] < len(item_lists[i])),
                key=lambda i: (done[i] + 0.5) / len(item_lists[i]))
        item_lists[i][done[i]]()
        done[i] += 1


def _attention_pipeline(n_q, n_streams, dv, q_of, k_rows_of, vt_cols_of, s_ref, p_ref,
                        finalize):
    mask = _chunk_mask()
    state = {}

    def slot(qi, st):
        return (qi % 2) * n_streams + st

    def pv_item(qi, st):
        kv_len = (qi + 1) * Q_TILE
        acc = jnp.dot(vt_cols_of(st)(0, kv_len), p_ref[slot(qi, st), 0:kv_len, :],
                      preferred_element_type=F32)
        state[qi, st]["acc"] = acc[0:dv, :]
        state[qi, st]["l"] = acc[dv:dv + 1, :]

    for t in range(n_q + 2):
        work = []
        for st in range(n_streams):
            if t < n_q:
                state[t, st] = {}
                work.append(_score_items(q_of(t, st), k_rows_of(st), (t + 1) * Q_TILE,
                                         s_ref.at[slot(t, st)], mask, state[t, st]))
            if 0 <= t - 1 < n_q:
                work.append(_exp_items(t * Q_TILE, s_ref.at[slot(t - 1, st)],
                                       p_ref.at[slot(t - 1, st)], state[t - 1, st]))
            if 0 <= t - 2 < n_q:
                work.append([functools.partial(pv_item, t - 2, st)])
        _emit_interleaved(work)
        if 0 <= t - 2 < n_q:
            finalize(t - 2, [(state[t - 2, st]["l"], state.pop((t - 2, st))["acc"])
                             for st in range(n_streams)])


def _store_transposed_values(v_ref, vt_ref, dv):
    n_heads = v_ref.shape[2] // dv
    stride = dv + ONES_ROWS
    ones = jnp.ones((ONES_ROWS, KV_TILE), BF16)
    for r0 in range(0, v_ref.shape[1], KV_TILE):
        blk_t = v_ref[0, r0:r0 + KV_TILE, :].astype(F32).T.astype(BF16)
        for h in range(n_heads):
            vt_ref[h * stride:h * stride + dv, r0:r0 + KV_TILE] = blk_t[h * dv:(h + 1) * dv, :]
            vt_ref[h * stride + dv:(h + 1) * stride, r0:r0 + KV_TILE] = ones


def _a_attn_kernel(lambda_init, q_ref, k_ref, v_ref, g_ref, lam_ref, subg_ref, o_ref,
                   kp_ref, vt_ref, s_ref, p_ref):
    s = q_ref.shape[1]
    lane = lax.broadcasted_iota(jnp.int32, (KV_TILE, LANES), 1)
    first_map = lane < A_HEAD_DIM
    for r0 in range(0, s, KV_TILE):
        kb = k_ref[0, r0:r0 + KV_TILE, :].astype(F32)
        kp_ref[0, r0:r0 + KV_TILE, :] = jnp.where(first_map, kb, 0.0).astype(BF16)
        kp_ref[1, r0:r0 + KV_TILE, :] = jnp.where(first_map, 0.0, kb).astype(BF16)
    dv = 2 * A_HEAD_DIM
    _store_transposed_values(v_ref, vt_ref, dv)

    lam = (jnp.exp(jnp.sum(lam_ref[0:1, :] * lam_ref[1:2, :], axis=1, keepdims=True))
           - jnp.exp(jnp.sum(lam_ref[2:3, :] * lam_ref[3:4, :], axis=1, keepdims=True))
           + lambda_init)
    out_gain = subg_ref[0:1, :] * (1.0 - lambda_init)

    def finalize(qi, outs):
        rows = slice(qi * Q_TILE, (qi + 1) * Q_TILE)
        (l1, acc1), (l2, acc2) = outs
        o_t = acc1 * (1.0 / l1) - acc2 * (lam / l2)
        ms = jnp.mean(o_t * o_t, axis=0, keepdims=True)
        o_t = o_t * lax.rsqrt(ms + SUBLN_EPS)
        o = o_t.T * out_gain * g_ref[0, rows, :].astype(F32)
        o_ref[0, rows, :] = o.astype(BF16)

    _attention_pipeline(
        s // Q_TILE, 2, dv,
        lambda qi, st: q_ref[0, qi * Q_TILE:(qi + 1) * Q_TILE, :],
        lambda st: (lambda r0, r1: kp_ref[st, r0:r1, :]),
        lambda st: (lambda c0, c1: vt_ref[:, c0:c1]),
        s_ref, p_ref, finalize)


def _a_attention(q, k, v, g, lam_rows, subg_rows, lambda_init):
    b, s, _ = q.shape
    spec = pl.BlockSpec((1, s, LANES), lambda bi, h: (bi, 0, h))
    small = pl.BlockSpec((8, LANES), lambda bi, h: (0, 0))
    return pl.pallas_call(
        functools.partial(_a_attn_kernel, lambda_init),
        grid=(b, A_HEADS),
        in_specs=[spec, spec, spec, spec, small, small],
        out_specs=spec,
        out_shape=jax.ShapeDtypeStruct((b, s, A_WIDTH), BF16),
        scratch_shapes=[pltpu.VMEM((2, s, LANES), BF16),
                        pltpu.VMEM((LANES + ONES_ROWS, s), BF16),
                        pltpu.VMEM((SCORE_SLOTS, s, Q_TILE), BF16),
                        pltpu.VMEM((SCORE_SLOTS, s, Q_TILE), BF16)],
        compiler_params=pltpu.CompilerParams(vmem_limit_bytes=VMEM_LIMIT),
        name="a_attention",
    )(q, k, v, g, lam_rows, subg_rows)


def _outproj_ln_kernel(o_ref, w_ref, x_ref, mod_ref, lng_ref, lnb_ref, y_ref):
    y = jnp.dot(o_ref[0], w_ref[...], preferred_element_type=F32)
    gate = mod_ref[0, 2:3, :]
    z = DEEPNORM_ALPHA * x_ref[0] + gate * y
    mu = jnp.mean(z, axis=1, keepdims=True)
    zc = z - mu
    var = jnp.mean(zc * zc, axis=1, keepdims=True)
    y_ref[0] = zc * lax.rsqrt(var + LN_EPS) * lng_ref[...] + lnb_ref[...]


def _outproj_ln(o, w_bf16, x, mod, ln_g, ln_b):
    b, s, d = x.shape
    width = o.shape[2]
    row = pl.BlockSpec((1, d), lambda bi, i: (0, 0))
    return pl.pallas_call(
        _outproj_ln_kernel,
        grid=(b, s // ROW_TILE),
        in_specs=[pl.BlockSpec((1, ROW_TILE, width), lambda bi, i: (bi, i, 0)),
                  pl.BlockSpec((width, d), lambda bi, i: (0, 0)),
                  pl.BlockSpec((1, ROW_TILE, d), lambda bi, i: (bi, i, 0)),
                  pl.BlockSpec((1, 3, d), lambda bi, i: (bi, 0, 0)),
                  row, row],
        out_specs=pl.BlockSpec((1, ROW_TILE, d), lambda bi, i: (bi, i, 0)),
        out_shape=jax.ShapeDtypeStruct((b, s, d), F32),
        compiler_params=pltpu.CompilerParams(vmem_limit_bytes=VMEM_LIMIT),
        name="outproj_ln",
    )(o, w_bf16, x, mod, ln_g.reshape(1, d), ln_b.reshape(1, d))


B_HEAD_PAD = LANES
B_QK_PAD = B_HEADS * B_HEAD_PAD
B_COL_KV = B_Q_RANK
B_COL_ROPE = B_Q_RANK + B_KV_RANK
B_COL_GATE = B_COL_ROPE + LANES
B_IN_PAD = B_COL_GATE + B_WIDTH
B_COL_CHUNK = 512


def _rms(x, g_row, eps):
    ms = jnp.mean(x * x, axis=1, keepdims=True)
    return x * lax.rsqrt(ms + eps) * g_row


def _b_inproj_kernel(x_ref, mod_ref, pos_ref, consts_ref, w1_ref, qg_ref, wuq_ref, kvg_ref,
                     wk_ref, wv_ref, q_ref, k_ref, v_ref, g_ref):
    i = pl.program_id(1)
    shift = mod_ref[0, 0:1, :]
    scale = mod_ref[0, 1:2, :]
    u = (x_ref[0] * (1.0 + scale) + shift).astype(BF16)
    cos_t, sin_up, sin_dn = _rope_tables(pos_ref, i, consts_ref)
    half = B_ROPE // 2
    qk_scale = (B_NOPE + B_ROPE) ** -0.5 * LOG2E

    lat = jnp.dot(u, w1_ref[:, 0:B_COL_GATE], preferred_element_type=F32)
    qn = _rms(lat[:, 0:B_Q_RANK], qg_ref[...], RMS_EPS).astype(BF16)
    kvn = _rms(lat[:, B_COL_KV:B_COL_ROPE], kvg_ref[...], RMS_EPS).astype(BF16)
    k_rope = _rope_apply(lat[:, B_COL_ROPE:B_COL_GATE], cos_t, sin_up, sin_dn, half)

    cos_q, up_q, dn_q = cos_t * qk_scale, sin_up * qk_scale, sin_dn * qk_scale
    for c0 in range(0, B_QK_PAD, B_COL_CHUNK):
        qc = jnp.dot(qn, wuq_ref[:, c0:c0 + B_COL_CHUNK], preferred_element_type=F32)
        kc = jnp.dot(kvn, wk_ref[:, c0:c0 + B_COL_CHUNK], preferred_element_type=F32)
        for g0 in range(0, B_COL_CHUNK, B_HEAD_PAD):
            cols = slice(c0 + g0, c0 + g0 + B_HEAD_PAD)
            q_ref[0, :, cols] = _rope_apply(qc[:, g0:g0 + B_HEAD_PAD], cos_q, up_q, dn_q,
                                            half).astype(BF16)
            k_ref[0, :, cols] = (kc[:, g0:g0 + B_HEAD_PAD] + k_rope).astype(BF16)
    v_ref[0] = jnp.dot(kvn, wv_ref[...], preferred_element_type=F32).astype(BF16)
    gate = jnp.dot(u, w1_ref[:, B_COL_GATE:B_IN_PAD], preferred_element_type=F32)
    g_ref[0] = _silu(gate).astype(BF16)


def _b_inproj(x, mod, pos3, consts, w1, qg, wuq, kvg, wk, wv):
    b, s, d = x.shape

    def full(a):
        return pl.BlockSpec(a.shape, lambda bi, i: (0,) * a.ndim)

    def rows(width):
        return pl.BlockSpec((1, ROW_TILE, width), lambda bi, i: (bi, i, 0))

    return pl.pallas_call(
        _b_inproj_kernel,
        grid=(b, s // ROW_TILE),
        in_specs=[rows(d),
                  pl.BlockSpec((1, 3, d), lambda bi, i: (bi, 0, 0)),
                  pl.BlockSpec((1, s // LANES, LANES), lambda bi, i: (bi, 0, 0)),
                  full(consts), full(w1), full(qg), full(wuq), full(kvg), full(wk), full(wv)],
        out_specs=[rows(B_QK_PAD), rows(B_QK_PAD), rows(B_WIDTH), rows(B_WIDTH)],
        out_shape=[jax.ShapeDtypeStruct((b, s, B_QK_PAD), BF16),
                   jax.ShapeDtypeStruct((b, s, B_QK_PAD), BF16),
                   jax.ShapeDtypeStruct((b, s, B_WIDTH), BF16),
                   jax.ShapeDtypeStruct((b, s, B_WIDTH), BF16)],
        compiler_params=pltpu.CompilerParams(vmem_limit_bytes=VMEM_LIMIT),
        name="b_inproj",
    )(x, mod, pos3, consts, w1, qg, wuq, kvg, wk, wv)


B_HEADS_PER_STEP = LANES // B_VDIM


def _b_attn_kernel(q_ref, k_ref, v_ref, g_ref, o_ref, vt_ref, s_ref, p_ref):
    s = q_ref.shape[1]
    _store_transposed_values(v_ref, vt_ref, B_VDIM)
    stride = B_VDIM + ONES_ROWS

    def finalize(qi, outs):
        rows = slice(qi * Q_TILE, (qi + 1) * Q_TILE)
        o_t = jnp.concatenate([acc * (1.0 / l) for l, acc in outs], axis=0)
        o_ref[0, rows, :] = (o_t.T * g_ref[0, rows, :].astype(F32)).astype(BF16)

    _attention_pipeline(
        s // Q_TILE, B_HEADS_PER_STEP, B_VDIM,
        lambda qi, st: q_ref[0, qi * Q_TILE:(qi + 1) * Q_TILE,
                             st * B_HEAD_PAD:(st + 1) * B_HEAD_PAD],
        lambda st: (lambda r0, r1: k_ref[0, r0:r1, st * B_HEAD_PAD:(st + 1) * B_HEAD_PAD]),
        lambda st: (lambda c0, c1: vt_ref[st * stride:(st + 1) * stride, c0:c1]),
        s_ref, p_ref, finalize)


def _b_attention(q, k, v, g):
    b, s, _ = q.shape
    assert SCORE_SLOTS == 2 * B_HEADS_PER_STEP
    qk_spec = pl.BlockSpec((1, s, B_HEADS_PER_STEP * B_HEAD_PAD), lambda bi, h: (bi, 0, h))
    spec = pl.BlockSpec((1, s, LANES), lambda bi, h: (bi, 0, h))
    return pl.pallas_call(
        _b_attn_kernel,
        grid=(b, B_HEADS // B_HEADS_PER_STEP),
        in_specs=[qk_spec, qk_spec, spec, spec],
        out_specs=spec,
        out_shape=jax.ShapeDtypeStruct((b, s, B_WIDTH), BF16),
        scratch_shapes=[pltpu.VMEM((B_HEADS_PER_STEP * (B_VDIM + ONES_ROWS), s), BF16),
                        pltpu.VMEM((SCORE_SLOTS, s, Q_TILE), BF16),
                        pltpu.VMEM((SCORE_SLOTS, s, Q_TILE), BF16)],
        compiler_params=pltpu.CompilerParams(vmem_limit_bytes=VMEM_LIMIT),
        name="b_attention",
    )(q, k, v, g)


def _pad_rows8(rows):
    out = jnp.zeros((8, LANES), F32)
    for r, vec in enumerate(rows):
        out = out.at[r, :vec.shape[0]].set(vec.astype(F32))
    return out


def _b_layouts(w_in, w_uq, w_ukv):
    d = w_in.shape[0]
    rope_cols = jnp.zeros((d, LANES), F32).at[:, B_NOPE:B_NOPE + B_ROPE].set(
        w_in[:, B_COL_ROPE:B_COL_ROPE + B_ROPE])
    w1 = jnp.concatenate([w_in[:, :B_COL_ROPE], rope_cols, w_in[:, B_COL_ROPE + B_ROPE:]], axis=1)
    uq = w_uq.reshape(B_Q_RANK, B_HEADS, B_NOPE + B_ROPE)
    uq = jnp.pad(uq, ((0, 0), (0, 0), (0, B_HEAD_PAD - B_NOPE - B_ROPE)))
    ukv = w_ukv.reshape(B_KV_RANK, B_HEADS, B_NOPE + B_VDIM)
    wk = jnp.pad(ukv[:, :, :B_NOPE], ((0, 0), (0, 0), (0, B_HEAD_PAD - B_NOPE)))
    wv = ukv[:, :, B_NOPE:]
    return (w1.astype(BF16), uq.reshape(B_Q_RANK, B_QK_PAD).astype(BF16),
            wk.reshape(B_KV_RANK, B_QK_PAD).astype(BF16),
            wv.reshape(B_KV_RANK, B_WIDTH).astype(BF16))


def kernel(x, c, positions, ada_w, ada_b, ln_g, ln_b, a_w_in, a_lambda_q1, a_lambda_k1,
           a_lambda_q2, a_lambda_k2, a_subln_g, a_w_out, b_w_in, b_q_norm_g, b_w_uq,
           b_kv_norm_g, b_w_ukv, b_w_out):
    b, s, d = x.shape
    assert d == D_MODEL and s % ROW_TILE == 0 and s % Q_TILE == 0 and Q_TILE == KV_TILE
    mod = _modulation(c, ada_w, ada_b).reshape(DEPTH, b, 3, d)
    pos3 = positions.reshape(b, s // LANES, LANES)

    lambda_init = 0.8 - 0.6 * math.exp(-0.3 * 0)
    q, k, v, g = _a_inproj(x, mod[0], pos3, _rope_consts(A_ROT, A_HEAD_DIM, 0),
                           a_w_in[0].astype(BF16))
    lam_rows = _pad_rows8([a_lambda_q1[0], a_lambda_k1[0], a_lambda_q2[0], a_lambda_k2[0]])
    o = _a_attention(q, k, v, g, lam_rows, _pad_rows8([a_subln_g[0]]), lambda_init)
    x = _outproj_ln(o, a_w_out[0].astype(BF16), x, mod[0], ln_g[0], ln_b[0])

    w1, wuq, wk, wv = _b_layouts(b_w_in[0], b_w_uq[0], b_w_ukv[0])
    q, k, v, g = _b_inproj(x, mod[1], pos3, _rope_consts(B_ROPE, B_HEAD_PAD, B_NOPE), w1,
                           b_q_norm_g[0].reshape(1, B_Q_RANK), wuq,
                           b_kv_norm_g[0].reshape(1, B_KV_RANK), wk, wv)
    o = _b_attention(q, k, v, g)
    return _outproj_ln(o, b_w_out[0].astype(BF16), x, mod[1], ln_g[1], ln_b[1])
```

```python
import functools
import math

import jax
import jax.numpy as jnp
import numpy as np
from jax import lax
from jax.experimental import pallas as pl
from jax.experimental.pallas import tpu as pltpu

F32 = jnp.float32
BF16 = jnp.bfloat16

D_MODEL = 1024
CHUNK = 64
ROPE_THETA = 500000.0

A_HEADS = 8
A_HEAD_DIM = 64
A_ROT = A_HEAD_DIM // 4
A_WIDTH = A_HEADS * 2 * A_HEAD_DIM

B_HEADS = 16
B_NOPE = 64
B_ROPE = 32
B_VDIM = 64
B_Q_RANK = 512
B_KV_RANK = 256
B_WIDTH = B_HEADS * B_VDIM

DEPTH = 2
DEEPNORM_ALPHA = (2.0 * DEPTH) ** 0.25
LN_EPS = 1e-5
RMS_EPS = 1e-6
SUBLN_EPS = 1e-5
NEG_INF = -1e30

LANES = 128
ROW_TILE = 512
Q_TILE = 256
KV_TILE = 256
SCORE_ROWS = 512
SCORE_DTYPE = jnp.float32
EMIT_ORDER = "fine"
ONES_ROWS = 0
A_HEADS_PER_STEP = 2
LOG2E = math.log2(math.e)
VMEM_LIMIT = 56 * 1024 * 1024

_NT = (((1,), (1,)), ((), ()))


def _silu(x):
    return x * (1.0 / (1.0 + jnp.exp(-x)))


def _mod_kernel(c_ref, w_ref, b_ref, o_ref):
    ca = _silu(c_ref[...])
    o_ref[0] = jnp.dot(ca, w_ref[0], precision=lax.Precision.HIGHEST,
                       preferred_element_type=F32) + b_ref[0]


def _modulation(c, ada_w, ada_b):
    depth, d, n3 = ada_w.shape
    b = c.shape[0]
    tn = 1024
    return pl.pallas_call(
        _mod_kernel,
        grid=(depth, n3 // tn),
        in_specs=[pl.BlockSpec((b, d), lambda i, j: (0, 0)),
                  pl.BlockSpec((1, d, tn), lambda i, j: (i, 0, j)),
                  pl.BlockSpec((1, 1, tn), lambda i, j: (i, 0, j))],
        out_specs=pl.BlockSpec((1, b, tn), lambda i, j: (i, 0, j)),
        out_shape=jax.ShapeDtypeStruct((depth, b, n3), F32),
        compiler_params=pltpu.CompilerParams(vmem_limit_bytes=VMEM_LIMIT),
        name="modulation",
    )(c, ada_w, ada_b.reshape(depth, 1, n3))


def _rope_tables(pos_ref, row_tile_idx, consts_ref):
    invf = consts_ref[0:1, :]
    cs, sn = [], []
    for r in range(ROW_TILE // LANES):
        row = pos_ref[0, pl.ds(row_tile_idx * (ROW_TILE // LANES) + r, 1), :].astype(F32)
        pos_col = jnp.broadcast_to(row, (LANES, LANES)).T
        ang = pos_col * invf
        cs.append(jnp.cos(ang))
        sn.append(jnp.sin(ang))
    cos_t = jnp.concatenate(cs, axis=0)
    sin_t = jnp.concatenate(sn, axis=0)
    return cos_t, sin_t * consts_ref[1:2, :], sin_t * consts_ref[2:3, :]


def _rope_apply(a, cos_t, sin_up, sin_dn, half):
    return (a * cos_t + pltpu.roll(a, LANES - half, 1) * sin_up
            + pltpu.roll(a, half, 1) * sin_dn)


def _rope_consts(rot_dim, group, offset):
    inv_freq = ROPE_THETA ** (-jnp.arange(0, rot_dim, 2, dtype=F32) / rot_dim)
    half = rot_dim // 2
    lane = np.arange(LANES)
    d = lane % group - offset
    rotated = (d >= 0) & (d < rot_dim)
    first = rotated & (d < half)
    second = rotated & (d >= half)
    idx = np.where(rotated, d % half, 0)
    invf = jnp.where(jnp.asarray(rotated), inv_freq[idx], 0.0)
    rows = jnp.stack([invf, jnp.asarray(-first.astype(np.float32)),
                      jnp.asarray(second.astype(np.float32))])
    return jnp.concatenate([rows, jnp.zeros((5, LANES), F32)], axis=0)


def _a_inproj_kernel(x_ref, mod_ref, pos_ref, consts_ref, w_ref, q_ref, k_ref, v_ref, g_ref):
    i = pl.program_id(1)
    shift = mod_ref[0, 0:1, :]
    scale = mod_ref[0, 1:2, :]
    u = (x_ref[0] * (1.0 + scale) + shift).astype(BF16)
    cos_t, sin_up, sin_dn = _rope_tables(pos_ref, i, consts_ref)
    half = A_ROT // 2
    qk_scale = A_HEAD_DIM ** -0.5 * LOG2E

    def rope_store(col0, o_ref, mult):
        acc = jnp.dot(u, w_ref[:, col0:col0 + A_WIDTH], preferred_element_type=F32)
        for g in range(A_WIDTH // LANES):
            a = acc[:, g * LANES:(g + 1) * LANES]
            r = _rope_apply(a, cos_t, sin_up, sin_dn, half)
            if mult != 1.0:
                r = r * mult
            o_ref[0, :, g * LANES:(g + 1) * LANES] = r.astype(BF16)

    rope_store(0, q_ref, qk_scale)
    rope_store(A_WIDTH, k_ref, 1.0)
    v_ref[0] = jnp.dot(u, w_ref[:, 2 * A_WIDTH:3 * A_WIDTH],
                       preferred_element_type=F32).astype(BF16)
    gate = jnp.dot(u, w_ref[:, 3 * A_WIDTH:4 * A_WIDTH], preferred_element_type=F32)
    g_ref[0] = _silu(gate).astype(BF16)


def _a_inproj(x, mod, pos3, consts, w_bf16):
    b, s, d = x.shape
    n = w_bf16.shape[1]
    out = jax.ShapeDtypeStruct((b, s, A_WIDTH), BF16)
    ospec = pl.BlockSpec((1, ROW_TILE, A_WIDTH), lambda bi, i: (bi, i, 0))
    return pl.pallas_call(
        _a_inproj_kernel,
        grid=(b, s // ROW_TILE),
        in_specs=[pl.BlockSpec((1, ROW_TILE, d), lambda bi, i: (bi, i, 0)),
                  pl.BlockSpec((1, 3, d), lambda bi, i: (bi, 0, 0)),
                  pl.BlockSpec((1, s // LANES, LANES), lambda bi, i: (bi, 0, 0)),
                  pl.BlockSpec((8, LANES), lambda bi, i: (0, 0)),
                  pl.BlockSpec((d, n), lambda bi, i: (0, 0))],
        out_specs=[ospec, ospec, ospec, ospec],
        out_shape=[out, out, out, out],
        compiler_params=pltpu.CompilerParams(vmem_limit_bytes=VMEM_LIMIT),
        name="a_inproj",
    )(x, mod, pos3, consts, w_bf16)


def _chunk_mask():
    krow = lax.broadcasted_iota(jnp.int32, (KV_TILE, Q_TILE), 0) // CHUNK
    qcol = lax.broadcasted_iota(jnp.int32, (KV_TILE, Q_TILE), 1) // CHUNK
    return krow <= qcol


def _score_items(q, k_rows, kv_len, s_ref, mask, out):
    diag = kv_len - KV_TILE

    def boundary():
        s_d = lax.dot_general(k_rows(diag, kv_len), q, _NT, preferred_element_type=F32)
        s_d = jnp.where(mask, s_d, NEG_INF).astype(SCORE_DTYPE)
        s_ref[diag:kv_len, :] = s_d
        out["m"] = jnp.max(s_d, axis=0, keepdims=True)

    def interior(r0, r1):
        s_o = lax.dot_general(k_rows(r0, r1), q, _NT,
                              preferred_element_type=F32).astype(SCORE_DTYPE)
        s_ref[r0:r1, :] = s_o
        out["m"] = jnp.maximum(out["m"], jnp.max(s_o, axis=0, keepdims=True))

    items = [boundary]
    for r0 in range(0, diag, SCORE_ROWS):
        items.append(functools.partial(interior, r0, min(r0 + SCORE_ROWS, diag)))
    return items


def _exp_items(kv_len, s_ref, p_ref, state):
    def block(r0):
        d = (s_ref[r0:r0 + KV_TILE, :] - state["m"]).astype(F32)
        p = jnp.exp2(d)
        if not ONES_ROWS:
            part = jnp.sum(p, axis=0, keepdims=True)
            state["l"] = part if "l" not in state else state["l"] + part
        p_ref[r0:r0 + KV_TILE, :] = p.astype(BF16)

    return [functools.partial(block, r0) for r0 in range(0, kv_len, KV_TILE)]


def _emit_interleaved(item_lists):
    item_lists = [items for items in item_lists if items]
    done = [0] * len(item_lists)
    for _ in range(sum(len(items) for items in item_lists)):
        i = min((i for i in range(len(item_lists)) if done[i] < len(item_lists[i])),
                key=lambda i: (done[i] + 0.5) / len(item_lists[i]))
        item_lists[i][done[i]]()
        done[i] += 1


def _attention_pipeline(n_q, n_streams, dv, q_of, k_rows_of, vt_cols_of, s_ref, p_ref,
                        finalize):
    mask = _chunk_mask()
    state = {}

    def slot(qi, st):
        return (qi % 2) * n_streams + st

    def pv_item(qi, st):
        kv_len = (qi + 1) * Q_TILE
        state[qi, st]["acc"] = jnp.dot(vt_cols_of(st)(0, kv_len),
                                       p_ref[slot(qi, st), 0:kv_len, :],
                                       preferred_element_type=F32)

    for t in range(n_q + 2):
        scores, exps, pvs = [], [], []
        for st in range(n_streams):
            if t < n_q:
                state[t, st] = {}
                scores.append(_score_items(q_of(t, st), k_rows_of(st), (t + 1) * Q_TILE,
                                           s_ref.at[slot(t, st)], mask, state[t, st]))
            if 0 <= t - 1 < n_q:
                exps.append(_exp_items(t * Q_TILE, s_ref.at[slot(t - 1, st)],
                                       p_ref.at[slot(t - 1, st)], state[t - 1, st]))
            if 0 <= t - 2 < n_q:
                pvs.append([functools.partial(pv_item, t - 2, st)])
        if EMIT_ORDER == "fine":
            _emit_interleaved(scores + exps + pvs)
        elif EMIT_ORDER == "coarse":
            _emit_interleaved(scores)
            _emit_interleaved(exps)
            _emit_interleaved(pvs)
        elif EMIT_ORDER == "mxu_first":
            _emit_interleaved(pvs)
            _emit_interleaved(scores)
            _emit_interleaved(exps)
        if 0 <= t - 2 < n_q:
            done = [state.pop((t - 2, st)) for st in range(n_streams)]
            if ONES_ROWS:
                finalize(t - 2, [(d["acc"][dv:dv + 1, :], d["acc"][0:dv, :]) for d in done])
            else:
                finalize(t - 2, [(d["l"], d["acc"]) for d in done])


def _store_transposed_values(v_ref, vt_ref, dv):
    n_heads = v_ref.shape[2] // dv
    stride = dv + ONES_ROWS
    ones = jnp.ones((ONES_ROWS, KV_TILE), BF16)
    for r0 in range(0, v_ref.shape[1], KV_TILE):
        blk_t = v_ref[0, r0:r0 + KV_TILE, :].astype(F32).T.astype(BF16)
        for h in range(n_heads):
            vt_ref[h * stride:h * stride + dv, r0:r0 + KV_TILE] = blk_t[h * dv:(h + 1) * dv, :]
            if ONES_ROWS:
                vt_ref[h * stride + dv:(h + 1) * stride, r0:r0 + KV_TILE] = ones


def _a_attn_kernel(lambda_init, q_ref, k_ref, v_ref, g_ref, lam_ref, subg_ref, o_ref,
                   kp_ref, vt_ref, s_ref, p_ref):
    s = q_ref.shape[1]
    lane = lax.broadcasted_iota(jnp.int32, (KV_TILE, LANES), 1)
    first_map = lane < A_HEAD_DIM
    for r0 in range(0, s, KV_TILE):
        for hh in range(A_HEADS_PER_STEP):
            kb = k_ref[0, r0:r0 + KV_TILE, hh * LANES:(hh + 1) * LANES].astype(F32)
            kp_ref[2 * hh, r0:r0 + KV_TILE, :] = jnp.where(first_map, kb, 0.0).astype(BF16)
            kp_ref[2 * hh + 1, r0:r0 + KV_TILE, :] = jnp.where(first_map, 0.0, kb).astype(BF16)
    dv = 2 * A_HEAD_DIM
    stride = dv + ONES_ROWS
    _store_transposed_values(v_ref, vt_ref, dv)

    lam = (jnp.exp(jnp.sum(lam_ref[0:1, :] * lam_ref[1:2, :], axis=1, keepdims=True))
           - jnp.exp(jnp.sum(lam_ref[2:3, :] * lam_ref[3:4, :], axis=1, keepdims=True))
           + lambda_init)
    out_gain = subg_ref[0:1, :] * (1.0 - lambda_init)

    def finalize(qi, outs):
        rows = slice(qi * Q_TILE, (qi + 1) * Q_TILE)
        for hh in range(A_HEADS_PER_STEP):
            cols = slice(hh * LANES, (hh + 1) * LANES)
            (l1, acc1), (l2, acc2) = outs[2 * hh:2 * hh + 2]
            o_t = acc1 * (1.0 / l1) - acc2 * (lam / l2)
            ms = jnp.mean(o_t * o_t, axis=0, keepdims=True)
            o_t = o_t * lax.rsqrt(ms + SUBLN_EPS)
            o = o_t.T * out_gain * g_ref[0, rows, cols].astype(F32)
            o_ref[0, rows, cols] = o.astype(BF16)

    _attention_pipeline(
        s // Q_TILE, 2 * A_HEADS_PER_STEP, dv,
        lambda qi, st: q_ref[0, qi * Q_TILE:(qi + 1) * Q_TILE,
                             (st // 2) * LANES:(st // 2 + 1) * LANES],
        lambda st: (lambda r0, r1: kp_ref[st, r0:r1, :]),
        lambda st: (lambda c0, c1: vt_ref[(st // 2) * stride:(st // 2 + 1) * stride, c0:c1]),
        s_ref, p_ref, finalize)


def _a_attention(q, k, v, g, lam_rows, subg_rows, lambda_init):
    b, s, _ = q.shape
    n_streams = 2 * A_HEADS_PER_STEP
    spec = pl.BlockSpec((1, s, A_HEADS_PER_STEP * LANES), lambda bi, h: (bi, 0, h))
    small = pl.BlockSpec((8, LANES), lambda bi, h: (0, 0))
    return pl.pallas_call(
        functools.partial(_a_attn_kernel, lambda_init),
        grid=(b, A_HEADS // A_HEADS_PER_STEP),
        in_specs=[spec, spec, spec, spec, small, small],
        out_specs=spec,
        out_shape=jax.ShapeDtypeStruct((b, s, A_WIDTH), BF16),
        scratch_shapes=[pltpu.VMEM((n_streams, s, LANES), BF16),
                        pltpu.VMEM((A_HEADS_PER_STEP * (LANES + ONES_ROWS), s), BF16),
                        pltpu.VMEM((2 * n_streams, s, Q_TILE), SCORE_DTYPE),
                        pltpu.VMEM((2 * n_streams, s, Q_TILE), BF16)],
        compiler_params=pltpu.CompilerParams(vmem_limit_bytes=VMEM_LIMIT),
        name="a_attention",
    )(q, k, v, g, lam_rows, subg_rows)


def _outproj_ln_kernel(o_ref, w_ref, x_ref, mod_ref, lng_ref, lnb_ref, y_ref):
    y = jnp.dot(o_ref[0], w_ref[...], preferred_element_type=F32)
    gate = mod_ref[0, 2:3, :]
    z = DEEPNORM_ALPHA * x_ref[0] + gate * y
    mu = jnp.mean(z, axis=1, keepdims=True)
    zc = z - mu
    var = jnp.mean(zc * zc, axis=1, keepdims=True)
    y_ref[0] = zc * lax.rsqrt(var + LN_EPS) * lng_ref[...] + lnb_ref[...]


def _outproj_ln(o, w_bf16, x, mod, ln_g, ln_b):
    b, s, d = x.shape
    width = o.shape[2]
    row = pl.BlockSpec((1, d), lambda bi, i: (0, 0))
    return pl.pallas_call(
        _outproj_ln_kernel,
        grid=(b, s // ROW_TILE),
        in_specs=[pl.BlockSpec((1, ROW_TILE, width), lambda bi, i: (bi, i, 0)),
                  pl.BlockSpec((width, d), lambda bi, i: (0, 0)),
                  pl.BlockSpec((1, ROW_TILE, d), lambda bi, i: (bi, i, 0)),
                  pl.BlockSpec((1, 3, d), lambda bi, i: (bi, 0, 0)),
                  row, row],
        out_specs=pl.BlockSpec((1, ROW_TILE, d), lambda bi, i: (bi, i, 0)),
        out_shape=jax.ShapeDtypeStruct((b, s, d), F32),
        compiler_params=pltpu.CompilerParams(vmem_limit_bytes=VMEM_LIMIT),
        name="outproj_ln",
    )(o, w_bf16, x, mod, ln_g.reshape(1, d), ln_b.reshape(1, d))


B_HEAD_PAD = LANES
B_QK_PAD = B_HEADS * B_HEAD_PAD
B_COL_KV = B_Q_RANK
B_COL_ROPE = B_Q_RANK + B_KV_RANK
B_COL_GATE = B_COL_ROPE + LANES
B_IN_PAD = B_COL_GATE + B_WIDTH
B_COL_CHUNK = 512


def _rms(x, g_row, eps):
    ms = jnp.mean(x * x, axis=1, keepdims=True)
    return x * lax.rsqrt(ms + eps) * g_row


def _b_inproj_kernel(x_ref, mod_ref, pos_ref, consts_ref, w1_ref, qg_ref, wuq_ref, kvg_ref,
                     wk_ref, wv_ref, q_ref, k_ref, v_ref, g_ref):
    i = pl.program_id(1)
    shift = mod_ref[0, 0:1, :]
    scale = mod_ref[0, 1:2, :]
    u = (x_ref[0] * (1.0 + scale) + shift).astype(BF16)
    cos_t, sin_up, sin_dn = _rope_tables(pos_ref, i, consts_ref)
    half = B_ROPE // 2
    qk_scale = (B_NOPE + B_ROPE) ** -0.5 * LOG2E

    lat = jnp.dot(u, w1_ref[:, 0:B_COL_GATE], preferred_element_type=F32)
    qn = _rms(lat[:, 0:B_Q_RANK], qg_ref[...], RMS_EPS).astype(BF16)
    kvn = _rms(lat[:, B_COL_KV:B_COL_ROPE], kvg_ref[...], RMS_EPS).astype(BF16)
    k_rope = _rope_apply(lat[:, B_COL_ROPE:B_COL_GATE], cos_t, sin_up, sin_dn, half)

    cos_q, up_q, dn_q = cos_t * qk_scale, sin_up * qk_scale, sin_dn * qk_scale
    for c0 in range(0, B_QK_PAD, B_COL_CHUNK):
        qc = jnp.dot(qn, wuq_ref[:, c0:c0 + B_COL_CHUNK], preferred_element_type=F32)
        kc = jnp.dot(kvn, wk_ref[:, c0:c0 + B_COL_CHUNK], preferred_element_type=F32)
        for g0 in range(0, B_COL_CHUNK, B_HEAD_PAD):
            cols = slice(c0 + g0, c0 + g0 + B_HEAD_PAD)
            q_ref[0, :, cols] = _rope_apply(qc[:, g0:g0 + B_HEAD_PAD], cos_q, up_q, dn_q,
                                            half).astype(BF16)
            k_ref[0, :, cols] = (kc[:, g0:g0 + B_HEAD_PAD] + k_rope).astype(BF16)
    v_ref[0] = jnp.dot(kvn, wv_ref[...], preferred_element_type=F32).astype(BF16)
    gate = jnp.dot(u, w1_ref[:, B_COL_GATE:B_IN_PAD], preferred_element_type=F32)
    g_ref[0] = _silu(gate).astype(BF16)


def _b_inproj(x, mod, pos3, consts, w1, qg, wuq, kvg, wk, wv):
    b, s, d = x.shape

    def full(a):
        return pl.BlockSpec(a.shape, lambda bi, i: (0,) * a.ndim)

    def rows(width):
        return pl.BlockSpec((1, ROW_TILE, width), lambda bi, i: (bi, i, 0))

    return pl.pallas_call(
        _b_inproj_kernel,
        grid=(b, s // ROW_TILE),
        in_specs=[rows(d),
                  pl.BlockSpec((1, 3, d), lambda bi, i: (bi, 0, 0)),
                  pl.BlockSpec((1, s // LANES, LANES), lambda bi, i: (bi, 0, 0)),
                  full(consts), full(w1), full(qg), full(wuq), full(kvg), full(wk), full(wv)],
        out_specs=[rows(B_QK_PAD), rows(B_QK_PAD), rows(B_WIDTH), rows(B_WIDTH)],
        out_shape=[jax.ShapeDtypeStruct((b, s, B_QK_PAD), BF16),
                   jax.ShapeDtypeStruct((b, s, B_QK_PAD), BF16),
                   jax.ShapeDtypeStruct((b, s, B_WIDTH), BF16),
                   jax.ShapeDtypeStruct((b, s, B_WIDTH), BF16)],
        compiler_params=pltpu.CompilerParams(vmem_limit_bytes=VMEM_LIMIT),
        name="b_inproj",
    )(x, mod, pos3, consts, w1, qg, wuq, kvg, wk, wv)


B_HEADS_PER_STEP = 2


def _b_attn_kernel(q_ref, k_ref, v_ref, g_ref, o_ref, vt_ref, s_ref, p_ref):
    s = q_ref.shape[1]
    _store_transposed_values(v_ref, vt_ref, B_VDIM)
    stride = B_VDIM + ONES_ROWS

    def finalize(qi, outs):
        rows = slice(qi * Q_TILE, (qi + 1) * Q_TILE)
        o_t = jnp.concatenate([acc * (1.0 / l) for l, acc in outs], axis=0)
        o_ref[0, rows, :] = (o_t.T * g_ref[0, rows, :].astype(F32)).astype(BF16)

    _attention_pipeline(
        s // Q_TILE, B_HEADS_PER_STEP, B_VDIM,
        lambda qi, st: q_ref[0, qi * Q_TILE:(qi + 1) * Q_TILE,
                             st * B_HEAD_PAD:(st + 1) * B_HEAD_PAD],
        lambda st: (lambda r0, r1: k_ref[0, r0:r1, st * B_HEAD_PAD:(st + 1) * B_HEAD_PAD]),
        lambda st: (lambda c0, c1: vt_ref[st * stride:(st + 1) * stride, c0:c1]),
        s_ref, p_ref, finalize)


def _b_attention(q, k, v, g):
    b, s, _ = q.shape
    qk_spec = pl.BlockSpec((1, s, B_HEADS_PER_STEP * B_HEAD_PAD), lambda bi, h: (bi, 0, h))
    spec = pl.BlockSpec((1, s, B_HEADS_PER_STEP * B_VDIM), lambda bi, h: (bi, 0, h))
    return pl.pallas_call(
        _b_attn_kernel,
        grid=(b, B_HEADS // B_HEADS_PER_STEP),
        in_specs=[qk_spec, qk_spec, spec, spec],
        out_specs=spec,
        out_shape=jax.ShapeDtypeStruct((b, s, B_WIDTH), BF16),
        scratch_shapes=[pltpu.VMEM((B_HEADS_PER_STEP * (B_VDIM + ONES_ROWS), s), BF16),
                        pltpu.VMEM((2 * B_HEADS_PER_STEP, s, Q_TILE), SCORE_DTYPE),
                        pltpu.VMEM((2 * B_HEADS_PER_STEP, s, Q_TILE), BF16)],
        compiler_params=pltpu.CompilerParams(vmem_limit_bytes=VMEM_LIMIT),
        name="b_attention",
    )(q, k, v, g)


def _pad_rows8(rows):
    out = jnp.zeros((8, LANES), F32)
    for r, vec in enumerate(rows):
        out = out.at[r, :vec.shape[0]].set(vec.astype(F32))
    return out


def _b_layouts(w_in, w_uq, w_ukv):
    d = w_in.shape[0]
    rope_cols = jnp.zeros((d, LANES), F32).at[:, B_NOPE:B_NOPE + B_ROPE].set(
        w_in[:, B_COL_ROPE:B_COL_ROPE + B_ROPE])
    w1 = jnp.concatenate([w_in[:, :B_COL_ROPE], rope_cols, w_in[:, B_COL_ROPE + B_ROPE:]], axis=1)
    uq = w_uq.reshape(B_Q_RANK, B_HEADS, B_NOPE + B_ROPE)
    uq = jnp.pad(uq, ((0, 0), (0, 0), (0, B_HEAD_PAD - B_NOPE - B_ROPE)))
    ukv = w_ukv.reshape(B_KV_RANK, B_HEADS, B_NOPE + B_VDIM)
    wk = jnp.pad(ukv[:, :, :B_NOPE], ((0, 0), (0, 0), (0, B_HEAD_PAD - B_NOPE)))
    wv = ukv[:, :, B_NOPE:]
    return (w1.astype(BF16), uq.reshape(B_Q_RANK, B_QK_PAD).astype(BF16),
            wk.reshape(B_KV_RANK, B_QK_PAD).astype(BF16),
            wv.reshape(B_KV_RANK, B_WIDTH).astype(BF16))


def kernel(x, c, positions, ada_w, ada_b, ln_g, ln_b, a_w_in, a_lambda_q1, a_lambda_k1,
           a_lambda_q2, a_lambda_k2, a_subln_g, a_w_out, b_w_in, b_q_norm_g, b_w_uq,
           b_kv_norm_g, b_w_ukv, b_w_out):
    b, s, d = x.shape
    assert d == D_MODEL and s % ROW_TILE == 0 and s % Q_TILE == 0 and Q_TILE == KV_TILE
    mod = _modulation(c, ada_w, ada_b).reshape(DEPTH, b, 3, d)
    pos3 = positions.reshape(b, s // LANES, LANES)

    lambda_init = 0.8 - 0.6 * math.exp(-0.3 * 0)
    q, k, v, g = _a_inproj(x, mod[0], pos3, _rope_consts(A_ROT, A_HEAD_DIM, 0),
                           a_w_in[0].astype(BF16))
    lam_rows = _pad_rows8([a_lambda_q1[0], a_lambda_k1[0], a_lambda_q2[0], a_lambda_k2[0]])
    o = _a_attention(q, k, v, g, lam_rows, _pad_rows8([a_subln_g[0]]), lambda_init)
    x = _outproj_ln(o, a_w_out[0].astype(BF16), x, mod[0], ln_g[0], ln_b[0])

    w1, wuq, wk, wv = _b_layouts(b_w_in[0], b_w_uq[0], b_w_ukv[0])
    q, k, v, g = _b_inproj(x, mod[1], pos3, _rope_consts(B_ROPE, B_HEAD_PAD, B_NOPE), w1,
                           b_q_norm_g[0].reshape(1, B_Q_RANK), wuq,
                           b_kv_norm_g[0].reshape(1, B_KV_RANK), wk, wv)
    o = _b_attention(q, k, v, g)
    return _outproj_ln(o, b_w_out[0].astype(BF16), x, mod[1], ln_g[1], ln_b[1])
```

```python
import functools
import math

import jax
import jax.numpy as jnp
import numpy as np
from jax import lax
from jax.experimental import pallas as pl
from jax.experimental.pallas import tpu as pltpu

F32 = jnp.float32
BF16 = jnp.bfloat16

D_MODEL = 1024
CHUNK = 64
ROPE_THETA = 500000.0

A_HEADS = 8
A_HEAD_DIM = 64
A_ROT = A_HEAD_DIM // 4
A_WIDTH = A_HEADS * 2 * A_HEAD_DIM

B_HEADS = 16
B_NOPE = 64
B_ROPE = 32
B_VDIM = 64
B_Q_RANK = 512
B_KV_RANK = 256
B_WIDTH = B_HEADS * B_VDIM

DEPTH = 2
DEEPNORM_ALPHA = (2.0 * DEPTH) ** 0.25
LN_EPS = 1e-5
RMS_EPS = 1e-6
SUBLN_EPS = 1e-5
NEG_INF = -1e30

LANES = 128
FREQ_ROWS = 16
ROW_TILE = 512
Q_TILE = 256
KV_TILE = 256
SCORE_ROWS = 512
SCORE_DTYPE = jnp.float32
EMIT_ORDER = "fine"
ONES_ROWS = 0
A_HEADS_PER_STEP = 2
ATTN_FLAGS = None
LOG2E = math.log2(math.e)
VMEM_LIMIT = 56 * 1024 * 1024

_NT = (((1,), (1,)), ((), ()))


def _silu(x):
    return x * (1.0 / (1.0 + jnp.exp(-x)))


def _mod_kernel(c_ref, w_ref, b_ref, o_ref):
    ca = _silu(c_ref[...])
    o_ref[0] = jnp.dot(ca, w_ref[0], precision=lax.Precision.HIGHEST,
                       preferred_element_type=F32) + b_ref[0]


def _modulation(c, ada_w, ada_b):
    depth, d, n3 = ada_w.shape
    b = c.shape[0]
    tn = 1024
    return pl.pallas_call(
        _mod_kernel,
        grid=(depth, n3 // tn),
        in_specs=[pl.BlockSpec((b, d), lambda i, j: (0, 0)),
                  pl.BlockSpec((1, d, tn), lambda i, j: (i, 0, j)),
                  pl.BlockSpec((1, 1, tn), lambda i, j: (i, 0, j))],
        out_specs=pl.BlockSpec((1, b, tn), lambda i, j: (i, 0, j)),
        out_shape=jax.ShapeDtypeStruct((depth, b, n3), F32),
        compiler_params=pltpu.CompilerParams(vmem_limit_bytes=VMEM_LIMIT),
        name="modulation",
    )(c, ada_w, ada_b.reshape(depth, 1, n3))


def _rope_tables(pos_ref, row_tile_idx, freq_ref, consts_ref):
    freq = freq_ref[...]
    cs, sn = [], []
    for r in range(ROW_TILE // LANES):
        pos = pos_ref[0, pl.ds(row_tile_idx * (ROW_TILE // LANES) + r, 1), :].astype(F32)
        ang = freq * pos
        reps = LANES // FREQ_ROWS
        cs.append(jnp.concatenate([jnp.cos(ang)] * reps, axis=0).T)
        sn.append(jnp.concatenate([jnp.sin(ang)] * reps, axis=0).T)
    cos_t = jnp.where(consts_ref[0:1, :] > 0.0, jnp.concatenate(cs, axis=0), 1.0)
    sin_t = jnp.concatenate(sn, axis=0)
    return cos_t, sin_t * consts_ref[1:2, :], sin_t * consts_ref[2:3, :]


def _rope_apply(a, cos_t, sin_up, sin_dn, half):
    return (a * cos_t + pltpu.roll(a, LANES - half, 1) * sin_up
            + pltpu.roll(a, half, 1) * sin_dn)


def _rope_consts(rot_dim, group, offset):
    half = rot_dim // 2
    assert FREQ_ROWS % half == 0 and offset % half == 0 and group % half == 0
    inv_freq = ROPE_THETA ** (-jnp.arange(0, rot_dim, 2, dtype=F32) / rot_dim)
    freq_rows = jnp.broadcast_to(jnp.tile(inv_freq, FREQ_ROWS // half)[:, None],
                                 (FREQ_ROWS, LANES))
    d = np.arange(LANES) % group - offset
    rotated = (d >= 0) & (d < rot_dim)
    first = rotated & (d < half)
    second = rotated & (d >= half)
    consts = np.zeros((8, LANES), np.float32)
    consts[0], consts[1], consts[2] = rotated, -first.astype(np.float32), second
    return freq_rows, jnp.asarray(consts)


def _a_inproj_kernel(x_ref, mod_ref, pos_ref, freq_ref, consts_ref, w_ref, q_ref, k_ref, v_ref,
                     g_ref):
    i = pl.program_id(1)
    shift = mod_ref[0, 0:1, :]
    scale = mod_ref[0, 1:2, :]
    u = (x_ref[0] * (1.0 + scale) + shift).astype(BF16)
    cos_t, sin_up, sin_dn = _rope_tables(pos_ref, i, freq_ref, consts_ref)
    half = A_ROT // 2
    qk_scale = A_HEAD_DIM ** -0.5 * LOG2E

    def rope_store(col0, o_ref, mult):
        acc = jnp.dot(u, w_ref[:, col0:col0 + A_WIDTH], preferred_element_type=F32)
        for g in range(A_WIDTH // LANES):
            a = acc[:, g * LANES:(g + 1) * LANES]
            r = _rope_apply(a, cos_t, sin_up, sin_dn, half)
            if mult != 1.0:
                r = r * mult
            o_ref[0, :, g * LANES:(g + 1) * LANES] = r.astype(BF16)

    rope_store(0, q_ref, qk_scale)
    rope_store(A_WIDTH, k_ref, 1.0)
    v_ref[0] = jnp.dot(u, w_ref[:, 2 * A_WIDTH:3 * A_WIDTH],
                       preferred_element_type=F32).astype(BF16)
    gate = jnp.dot(u, w_ref[:, 3 * A_WIDTH:4 * A_WIDTH], preferred_element_type=F32)
    g_ref[0] = _silu(gate).astype(BF16)


def _a_inproj(x, mod, pos3, rope, w_bf16):
    b, s, d = x.shape
    freq, consts = rope
    n = w_bf16.shape[1]
    out = jax.ShapeDtypeStruct((b, s, A_WIDTH), BF16)
    ospec = pl.BlockSpec((1, ROW_TILE, A_WIDTH), lambda bi, i: (bi, i, 0))
    return pl.pallas_call(
        _a_inproj_kernel,
        grid=(b, s // ROW_TILE),
        in_specs=[pl.BlockSpec((1, ROW_TILE, d), lambda bi, i: (bi, i, 0)),
                  pl.BlockSpec((1, 3, d), lambda bi, i: (bi, 0, 0)),
                  pl.BlockSpec((1, s // LANES, LANES), lambda bi, i: (bi, 0, 0)),
                  pl.BlockSpec((FREQ_ROWS, LANES), lambda bi, i: (0, 0)),
                  pl.BlockSpec((8, LANES), lambda bi, i: (0, 0)),
                  pl.BlockSpec((d, n), lambda bi, i: (0, 0))],
        out_specs=[ospec, ospec, ospec, ospec],
        out_shape=[out, out, out, out],
        compiler_params=pltpu.CompilerParams(vmem_limit_bytes=VMEM_LIMIT),
        name="a_inproj",
    )(x, mod, pos3, freq, consts, w_bf16)


def _chunk_mask():
    krow = lax.broadcasted_iota(jnp.int32, (KV_TILE, Q_TILE), 0) // CHUNK
    qcol = lax.broadcasted_iota(jnp.int32, (KV_TILE, Q_TILE), 1) // CHUNK
    return krow <= qcol


def _score_items(q, k_rows, kv_len, s_ref, mask, out):
    diag = kv_len - KV_TILE

    def boundary():
        s_d = lax.dot_general(k_rows(diag, kv_len), q, _NT, preferred_element_type=F32)
        s_d = jnp.where(mask, s_d, NEG_INF).astype(SCORE_DTYPE)
        s_ref[diag:kv_len, :] = s_d
        out["m"] = jnp.max(s_d, axis=0, keepdims=True)

    def interior(r0, r1):
        s_o = lax.dot_general(k_rows(r0, r1), q, _NT,
                              preferred_element_type=F32).astype(SCORE_DTYPE)
        s_ref[r0:r1, :] = s_o
        out["m"] = jnp.maximum(out["m"], jnp.max(s_o, axis=0, keepdims=True))

    items = [boundary]
    for r0 in range(0, diag, SCORE_ROWS):
        items.append(functools.partial(interior, r0, min(r0 + SCORE_ROWS, diag)))
    return items


def _exp_items(kv_len, s_ref, p_ref, state):
    def block(r0):
        d = (s_ref[r0:r0 + KV_TILE, :] - state["m"]).astype(F32)
        p = jnp.exp2(d)
        if not ONES_ROWS:
            part = jnp.sum(p, axis=0, keepdims=True)
            state["l"] = part if "l" not in state else state["l"] + part
        p_ref[r0:r0 + KV_TILE, :] = p.astype(BF16)

    return [functools.partial(block, r0) for r0 in range(0, kv_len, KV_TILE)]


def _emit_interleaved(item_lists):
    item_lists = [items for items in item_lists if items]
    done = [0] * len(item_lists)
    for _ in range(sum(len(items) for items in item_lists)):
        i = min((i for i in range(len(item_lists)) if done[i] < len(item_lists[i])),
                key=lambda i: (done[i] + 0.5) / len(item_lists[i]))
        item_lists[i][done[i]]()
        done[i] += 1


def _attention_pipeline(n_q, n_streams, dv, q_of, k_rows_of, vt_cols_of, s_ref, p_ref,
                        finalize):
    mask = _chunk_mask()
    state = {}

    def slot(qi, st):
        return (qi % 2) * n_streams + st

    def pv_item(qi, st):
        kv_len = (qi + 1) * Q_TILE
        state[qi, st]["acc"] = jnp.dot(vt_cols_of(st)(0, kv_len),
                                       p_ref[slot(qi, st), 0:kv_len, :],
                                       preferred_element_type=F32)

    for t in range(n_q + 2):
        scores, exps, pvs = [], [], []
        for st in range(n_streams):
            if t < n_q:
                state[t, st] = {}
                scores.append(_score_items(q_of(t, st), k_rows_of(st), (t + 1) * Q_TILE,
                                           s_ref.at[slot(t, st)], mask, state[t, st]))
            if 0 <= t - 1 < n_q:
                exps.append(_exp_items(t * Q_TILE, s_ref.at[slot(t - 1, st)],
                                       p_ref.at[slot(t - 1, st)], state[t - 1, st]))
            if 0 <= t - 2 < n_q:
                pvs.append([functools.partial(pv_item, t - 2, st)])
        if EMIT_ORDER == "fine":
            _emit_interleaved(scores + exps + pvs)
        elif EMIT_ORDER == "coarse":
            _emit_interleaved(scores)
            _emit_interleaved(exps)
            _emit_interleaved(pvs)
        elif EMIT_ORDER == "mxu_first":
            _emit_interleaved(pvs)
            _emit_interleaved(scores)
            _emit_interleaved(exps)
        if 0 <= t - 2 < n_q:
            done = [state.pop((t - 2, st)) for st in range(n_streams)]
            if ONES_ROWS:
                finalize(t - 2, [(d["acc"][dv:dv + 1, :], d["acc"][0:dv, :]) for d in done])
            else:
                finalize(t - 2, [(d["l"], d["acc"]) for d in done])


def _store_transposed_values(v_ref, vt_ref, dv):
    n_heads = v_ref.shape[2] // dv
    stride = dv + ONES_ROWS
    ones = jnp.ones((ONES_ROWS, KV_TILE), BF16)
    for r0 in range(0, v_ref.shape[1], KV_TILE):
        blk_t = v_ref[0, r0:r0 + KV_TILE, :].astype(F32).T.astype(BF16)
        for h in range(n_heads):
            vt_ref[h * stride:h * stride + dv, r0:r0 + KV_TILE] = blk_t[h * dv:(h + 1) * dv, :]
            if ONES_ROWS:
                vt_ref[h * stride + dv:(h + 1) * stride, r0:r0 + KV_TILE] = ones


def _a_attn_kernel(lambda_init, q_ref, k_ref, v_ref, g_ref, lam_ref, subg_ref, o_ref,
                   kp_ref, vt_ref, s_ref, p_ref):
    s = q_ref.shape[1]
    lane = lax.broadcasted_iota(jnp.int32, (KV_TILE, LANES), 1)
    first_map = lane < A_HEAD_DIM
    for r0 in range(0, s, KV_TILE):
        for hh in range(A_HEADS_PER_STEP):
            kb = k_ref[0, r0:r0 + KV_TILE, hh * LANES:(hh + 1) * LANES].astype(F32)
            kp_ref[2 * hh, r0:r0 + KV_TILE, :] = jnp.where(first_map, kb, 0.0).astype(BF16)
            kp_ref[2 * hh + 1, r0:r0 + KV_TILE, :] = jnp.where(first_map, 0.0, kb).astype(BF16)
    dv = 2 * A_HEAD_DIM
    stride = dv + ONES_ROWS
    _store_transposed_values(v_ref, vt_ref, dv)

    lam = (jnp.exp(jnp.sum(lam_ref[0:1, :] * lam_ref[1:2, :], axis=1, keepdims=True))
           - jnp.exp(jnp.sum(lam_ref[2:3, :] * lam_ref[3:4, :], axis=1, keepdims=True))
           + lambda_init)
    out_gain = subg_ref[0:1, :] * (1.0 - lambda_init)

    def finalize(qi, outs):
        rows = slice(qi * Q_TILE, (qi + 1) * Q_TILE)
        for hh in range(A_HEADS_PER_STEP):
            cols = slice(hh * LANES, (hh + 1) * LANES)
            (l1, acc1), (l2, acc2) = outs[2 * hh:2 * hh + 2]
            o_t = acc1 * (1.0 / l1) - acc2 * (lam / l2)
            ms = jnp.mean(o_t * o_t, axis=0, keepdims=True)
            o_t = o_t * lax.rsqrt(ms + SUBLN_EPS)
            o = o_t.T * out_gain * g_ref[0, rows, cols].astype(F32)
            o_ref[0, rows, cols] = o.astype(BF16)

    _attention_pipeline(
        s // Q_TILE, 2 * A_HEADS_PER_STEP, dv,
        lambda qi, st: q_ref[0, qi * Q_TILE:(qi + 1) * Q_TILE,
                             (st // 2) * LANES:(st // 2 + 1) * LANES],
        lambda st: (lambda r0, r1: kp_ref[st, r0:r1, :]),
        lambda st: (lambda c0, c1: vt_ref[(st // 2) * stride:(st // 2 + 1) * stride, c0:c1]),
        s_ref, p_ref, finalize)


def _a_attention(q, k, v, g, lam_rows, subg_rows, lambda_init):
    b, s, _ = q.shape
    n_streams = 2 * A_HEADS_PER_STEP
    spec = pl.BlockSpec((1, s, A_HEADS_PER_STEP * LANES), lambda bi, h: (bi, 0, h))
    small = pl.BlockSpec((8, LANES), lambda bi, h: (0, 0))
    return pl.pallas_call(
        functools.partial(_a_attn_kernel, lambda_init),
        grid=(b, A_HEADS // A_HEADS_PER_STEP),
        in_specs=[spec, spec, spec, spec, small, small],
        out_specs=spec,
        out_shape=jax.ShapeDtypeStruct((b, s, A_WIDTH), BF16),
        scratch_shapes=[pltpu.VMEM((n_streams, s, LANES), BF16),
                        pltpu.VMEM((A_HEADS_PER_STEP * (LANES + ONES_ROWS), s), BF16),
                        pltpu.VMEM((2 * n_streams, s, Q_TILE), SCORE_DTYPE),
                        pltpu.VMEM((2 * n_streams, s, Q_TILE), BF16)],
        compiler_params=pltpu.CompilerParams(vmem_limit_bytes=VMEM_LIMIT, flags=ATTN_FLAGS),
        name="a_attention",
    )(q, k, v, g, lam_rows, subg_rows)


def _outproj_ln_kernel(o_ref, w_ref, x_ref, mod_ref, lng_ref, lnb_ref, y_ref):
    y = jnp.dot(o_ref[0], w_ref[...], preferred_element_type=F32)
    gate = mod_ref[0, 2:3, :]
    z = DEEPNORM_ALPHA * x_ref[0] + gate * y
    mu = jnp.mean(z, axis=1, keepdims=True)
    zc = z - mu
    var = jnp.mean(zc * zc, axis=1, keepdims=True)
    y_ref[0] = zc * lax.rsqrt(var + LN_EPS) * lng_ref[...] + lnb_ref[...]


def _outproj_ln(o, w_bf16, x, mod, ln_g, ln_b):
    b, s, d = x.shape
    width = o.shape[2]
    row = pl.BlockSpec((1, d), lambda bi, i: (0, 0))
    return pl.pallas_call(
        _outproj_ln_kernel,
        grid=(b, s // ROW_TILE),
        in_specs=[pl.BlockSpec((1, ROW_TILE, width), lambda bi, i: (bi, i, 0)),
                  pl.BlockSpec((width, d), lambda bi, i: (0, 0)),
                  pl.BlockSpec((1, ROW_TILE, d), lambda bi, i: (bi, i, 0)),
                  pl.BlockSpec((1, 3, d), lambda bi, i: (bi, 0, 0)),
                  row, row],
        out_specs=pl.BlockSpec((1, ROW_TILE, d), lambda bi, i: (bi, i, 0)),
        out_shape=jax.ShapeDtypeStruct((b, s, d), F32),
        compiler_params=pltpu.CompilerParams(vmem_limit_bytes=VMEM_LIMIT),
        name="outproj_ln",
    )(o, w_bf16, x, mod, ln_g.reshape(1, d), ln_b.reshape(1, d))


B_HEAD_PAD = LANES
B_QK_PAD = B_HEADS * B_HEAD_PAD
B_COL_KV = B_Q_RANK
B_COL_ROPE = B_Q_RANK + B_KV_RANK
B_COL_GATE = B_COL_ROPE + LANES
B_IN_PAD = B_COL_GATE + B_WIDTH
B_COL_CHUNK = 512


def _rms(x, g_row, eps):
    ms = jnp.mean(x * x, axis=1, keepdims=True)
    return x * lax.rsqrt(ms + eps) * g_row


def _b_inproj_kernel(x_ref, mod_ref, pos_ref, freq_ref, consts_ref, w1_ref, qg_ref, wuq_ref, kvg_ref,
                     wk_ref, wv_ref, q_ref, k_ref, v_ref, g_ref):
    i = pl.program_id(1)
    shift = mod_ref[0, 0:1, :]
    scale = mod_ref[0, 1:2, :]
    u = (x_ref[0] * (1.0 + scale) + shift).astype(BF16)
    cos_t, sin_up, sin_dn = _rope_tables(pos_ref, i, freq_ref, consts_ref)
    half = B_ROPE // 2
    qk_scale = (B_NOPE + B_ROPE) ** -0.5 * LOG2E

    lat = jnp.dot(u, w1_ref[:, 0:B_COL_GATE], preferred_element_type=F32)
    qn = _rms(lat[:, 0:B_Q_RANK], qg_ref[...], RMS_EPS).astype(BF16)
    kvn = _rms(lat[:, B_COL_KV:B_COL_ROPE], kvg_ref[...], RMS_EPS).astype(BF16)
    k_rope = _rope_apply(lat[:, B_COL_ROPE:B_COL_GATE], cos_t, sin_up, sin_dn, half)

    cos_q, up_q, dn_q = cos_t * qk_scale, sin_up * qk_scale, sin_dn * qk_scale
    for c0 in range(0, B_QK_PAD, B_COL_CHUNK):
        qc = jnp.dot(qn, wuq_ref[:, c0:c0 + B_COL_CHUNK], preferred_element_type=F32)
        kc = jnp.dot(kvn, wk_ref[:, c0:c0 + B_COL_CHUNK], preferred_element_type=F32)
        for g0 in range(0, B_COL_CHUNK, B_HEAD_PAD):
            cols = slice(c0 + g0, c0 + g0 + B_HEAD_PAD)
            q_ref[0, :, cols] = _rope_apply(qc[:, g0:g0 + B_HEAD_PAD], cos_q, up_q, dn_q,
                                            half).astype(BF16)
            k_ref[0, :, cols] = (kc[:, g0:g0 + B_HEAD_PAD] + k_rope).astype(BF16)
    v_ref[0] = jnp.dot(kvn, wv_ref[...], preferred_element_type=F32).astype(BF16)
    gate = jnp.dot(u, w1_ref[:, B_COL_GATE:B_IN_PAD], preferred_element_type=F32)
    g_ref[0] = _silu(gate).astype(BF16)


def _b_inproj(x, mod, pos3, rope, w1, qg, wuq, kvg, wk, wv):
    b, s, d = x.shape
    freq, consts = rope

    def full(a):
        return pl.BlockSpec(a.shape, lambda bi, i: (0,) * a.ndim)

    def rows(width):
        return pl.BlockSpec((1, ROW_TILE, width), lambda bi, i: (bi, i, 0))

    return pl.pallas_call(
        _b_inproj_kernel,
        grid=(b, s // ROW_TILE),
        in_specs=[rows(d),
                  pl.BlockSpec((1, 3, d), lambda bi, i: (bi, 0, 0)),
                  pl.BlockSpec((1, s // LANES, LANES), lambda bi, i: (bi, 0, 0)),
                  full(freq), full(consts), full(w1), full(qg), full(wuq), full(kvg), full(wk),
                  full(wv)],
        out_specs=[rows(B_QK_PAD), rows(B_QK_PAD), rows(B_WIDTH), rows(B_WIDTH)],
        out_shape=[jax.ShapeDtypeStruct((b, s, B_QK_PAD), BF16),
                   jax.ShapeDtypeStruct((b, s, B_QK_PAD), BF16),
                   jax.ShapeDtypeStruct((b, s, B_WIDTH), BF16),
                   jax.ShapeDtypeStruct((b, s, B_WIDTH), BF16)],
        compiler_params=pltpu.CompilerParams(vmem_limit_bytes=VMEM_LIMIT),
        name="b_inproj",
    )(x, mod, pos3, freq, consts, w1, qg, wuq, kvg, wk, wv)


B_HEADS_PER_STEP = 4


def _b_attn_kernel(q_ref, k_ref, v_ref, g_ref, o_ref, vt_ref, s_ref, p_ref):
    s = q_ref.shape[1]
    _store_transposed_values(v_ref, vt_ref, B_VDIM)
    stride = B_VDIM + ONES_ROWS

    def finalize(qi, outs):
        rows = slice(qi * Q_TILE, (qi + 1) * Q_TILE)
        o_t = jnp.concatenate([acc * (1.0 / l) for l, acc in outs], axis=0)
        o_ref[0, rows, :] = (o_t.T * g_ref[0, rows, :].astype(F32)).astype(BF16)

    _attention_pipeline(
        s // Q_TILE, B_HEADS_PER_STEP, B_VDIM,
        lambda qi, st: q_ref[0, qi * Q_TILE:(qi + 1) * Q_TILE,
                             st * B_HEAD_PAD:(st + 1) * B_HEAD_PAD],
        lambda st: (lambda r0, r1: k_ref[0, r0:r1, st * B_HEAD_PAD:(st + 1) * B_HEAD_PAD]),
        lambda st: (lambda c0, c1: vt_ref[st * stride:(st + 1) * stride, c0:c1]),
        s_ref, p_ref, finalize)


def _b_attention(q, k, v, g):
    b, s, _ = q.shape
    qk_spec = pl.BlockSpec((1, s, B_HEADS_PER_STEP * B_HEAD_PAD), lambda bi, h: (bi, 0, h))
    spec = pl.BlockSpec((1, s, B_HEADS_PER_STEP * B_VDIM), lambda bi, h: (bi, 0, h))
    return pl.pallas_call(
        _b_attn_kernel,
        grid=(b, B_HEADS // B_HEADS_PER_STEP),
        in_specs=[qk_spec, qk_spec, spec, spec],
        out_specs=spec,
        out_shape=jax.ShapeDtypeStruct((b, s, B_WIDTH), BF16),
        scratch_shapes=[pltpu.VMEM((B_HEADS_PER_STEP * (B_VDIM + ONES_ROWS), s), BF16),
                        pltpu.VMEM((2 * B_HEADS_PER_STEP, s, Q_TILE), SCORE_DTYPE),
                        pltpu.VMEM((2 * B_HEADS_PER_STEP, s, Q_TILE), BF16)],
        compiler_params=pltpu.CompilerParams(vmem_limit_bytes=VMEM_LIMIT, flags=ATTN_FLAGS),
        name="b_attention",
    )(q, k, v, g)


def _pad_rows8(rows):
    out = jnp.zeros((8, LANES), F32)
    for r, vec in enumerate(rows):
        out = out.at[r, :vec.shape[0]].set(vec.astype(F32))
    return out


def _b_layouts(w_in, w_uq, w_ukv):
    d = w_in.shape[0]
    rope_cols = jnp.zeros((d, LANES), F32).at[:, B_NOPE:B_NOPE + B_ROPE].set(
        w_in[:, B_COL_ROPE:B_COL_ROPE + B_ROPE])
    w1 = jnp.concatenate([w_in[:, :B_COL_ROPE], rope_cols, w_in[:, B_COL_ROPE + B_ROPE:]], axis=1)
    uq = w_uq.reshape(B_Q_RANK, B_HEADS, B_NOPE + B_ROPE)
    uq = jnp.pad(uq, ((0, 0), (0, 0), (0, B_HEAD_PAD - B_NOPE - B_ROPE)))
    ukv = w_ukv.reshape(B_KV_RANK, B_HEADS, B_NOPE + B_VDIM)
    wk = jnp.pad(ukv[:, :, :B_NOPE], ((0, 0), (0, 0), (0, B_HEAD_PAD - B_NOPE)))
    wv = ukv[:, :, B_NOPE:]
    return (w1.astype(BF16), uq.reshape(B_Q_RANK, B_QK_PAD).astype(BF16),
            wk.reshape(B_KV_RANK, B_QK_PAD).astype(BF16),
            wv.reshape(B_KV_RANK, B_WIDTH).astype(BF16))


def kernel(x, c, positions, ada_w, ada_b, ln_g, ln_b, a_w_in, a_lambda_q1, a_lambda_k1,
           a_lambda_q2, a_lambda_k2, a_subln_g, a_w_out, b_w_in, b_q_norm_g, b_w_uq,
           b_kv_norm_g, b_w_ukv, b_w_out):
    b, s, d = x.shape
    assert d == D_MODEL and s % ROW_TILE == 0 and s % Q_TILE == 0 and Q_TILE == KV_TILE
    mod = _modulation(c, ada_w, ada_b).reshape(DEPTH, b, 3, d)
    pos3 = positions.reshape(b, s // LANES, LANES)

    lambda_init = 0.8 - 0.6 * math.exp(-0.3 * 0)
    q, k, v, g = _a_inproj(x, mod[0], pos3, _rope_consts(A_ROT, A_HEAD_DIM, 0),
                           a_w_in[0].astype(BF16))
    lam_rows = _pad_rows8([a_lambda_q1[0], a_lambda_k1[0], a_lambda_q2[0], a_lambda_k2[0]])
    o = _a_attention(q, k, v, g, lam_rows, _pad_rows8([a_subln_g[0]]), lambda_init)
    x = _outproj_ln(o, a_w_out[0].astype(BF16), x, mod[0], ln_g[0], ln_b[0])

    w1, wuq, wk, wv = _b_layouts(b_w_in[0], b_w_uq[0], b_w_ukv[0])
    q, k, v, g = _b_inproj(x, mod[1], pos3, _rope_consts(B_ROPE, B_HEAD_PAD, B_NOPE), w1,
                           b_q_norm_g[0].reshape(1, B_Q_RANK), wuq,
                           b_kv_norm_g[0].reshape(1, B_KV_RANK), wk, wv)
    o = _b_attention(q, k, v, g)
    return _outproj_ln(o, b_w_out[0].astype(BF16), x, mod[1], ln_g[1], ln_b[1])
```

```python
import functools
import math

import jax
import jax.numpy as jnp
import numpy as np
from jax import lax
from jax.experimental import pallas as pl
from jax.experimental.pallas import tpu as pltpu

F32 = jnp.float32
BF16 = jnp.bfloat16

D_MODEL = 1024
CHUNK = 64
ROPE_THETA = 500000.0

A_HEADS = 8
A_HEAD_DIM = 64
A_ROT = A_HEAD_DIM // 4
A_WIDTH = A_HEADS * 2 * A_HEAD_DIM

B_HEADS = 16
B_NOPE = 64
B_ROPE = 32
B_VDIM = 64
B_Q_RANK = 512
B_KV_RANK = 256
B_WIDTH = B_HEADS * B_VDIM

DEPTH = 2
DEEPNORM_ALPHA = (2.0 * DEPTH) ** 0.25
LN_EPS = 1e-5
RMS_EPS = 1e-6
SUBLN_EPS = 1e-5
NEG_INF = -1e30

LANES = 128
FREQ_ROWS = 16
ROW_TILE = 512
Q_TILE = 256
KV_TILE = 256
SCORE_ROWS = 512
SCORE_DTYPE = jnp.float32
EMIT_ORDER = "fine"
ONES_ROWS = 0
A_HEADS_PER_STEP = 2
ATTN_FLAGS = None
LOG2E = math.log2(math.e)
VMEM_LIMIT = 56 * 1024 * 1024

_NT = (((1,), (1,)), ((), ()))


def _silu(x):
    return x * (1.0 / (1.0 + jnp.exp(-x)))


def _mod_kernel(c_ref, w_ref, b_ref, o_ref):
    ca = _silu(c_ref[...]).astype(BF16)
    o_ref[0] = jnp.dot(ca, w_ref[0].astype(BF16), preferred_element_type=F32) + b_ref[0]


def _modulation(c, ada_w, ada_b):
    depth, d, n3 = ada_w.shape
    b = c.shape[0]
    tn = 1024
    return pl.pallas_call(
        _mod_kernel,
        grid=(depth, n3 // tn),
        in_specs=[pl.BlockSpec((b, d), lambda i, j: (0, 0)),
                  pl.BlockSpec((1, d, tn), lambda i, j: (i, 0, j)),
                  pl.BlockSpec((1, 1, tn), lambda i, j: (i, 0, j))],
        out_specs=pl.BlockSpec((1, b, tn), lambda i, j: (i, 0, j)),
        out_shape=jax.ShapeDtypeStruct((depth, b, n3), F32),
        compiler_params=pltpu.CompilerParams(vmem_limit_bytes=VMEM_LIMIT),
        name="modulation",
    )(c, ada_w, ada_b.reshape(depth, 1, n3))


def _rope_tables(pos_ref, row_tile_idx, freq_ref, consts_ref):
    freq = freq_ref[...]
    cs, sn = [], []
    for r in range(ROW_TILE // LANES):
        pos = pos_ref[0, pl.ds(row_tile_idx * (ROW_TILE // LANES) + r, 1), :].astype(F32)
        ang = freq * pos
        reps = LANES // FREQ_ROWS
        cs.append(jnp.concatenate([jnp.cos(ang)] * reps, axis=0).T)
        sn.append(jnp.concatenate([jnp.sin(ang)] * reps, axis=0).T)
    cos_t = jnp.where(consts_ref[0:1, :] > 0.0, jnp.concatenate(cs, axis=0), 1.0)
    sin_t = jnp.concatenate(sn, axis=0)
    return cos_t, sin_t * consts_ref[1:2, :], sin_t * consts_ref[2:3, :]


def _rope_apply(a, cos_t, sin_up, sin_dn, half):
    return (a * cos_t + pltpu.roll(a, LANES - half, 1) * sin_up
            + pltpu.roll(a, half, 1) * sin_dn)


def _rope_consts(rot_dim, group, offset):
    half = rot_dim // 2
    assert FREQ_ROWS % half == 0 and offset % half == 0 and group % half == 0
    inv_freq = ROPE_THETA ** (-jnp.arange(0, rot_dim, 2, dtype=F32) / rot_dim)
    freq_rows = jnp.broadcast_to(jnp.tile(inv_freq, FREQ_ROWS // half)[:, None],
                                 (FREQ_ROWS, LANES))
    d = np.arange(LANES) % group - offset
    rotated = (d >= 0) & (d < rot_dim)
    first = rotated & (d < half)
    second = rotated & (d >= half)
    consts = np.zeros((8, LANES), np.float32)
    consts[0], consts[1], consts[2] = rotated, -first.astype(np.float32), second
    return freq_rows, jnp.asarray(consts)


def _a_inproj_kernel(x_ref, mod_ref, pos_ref, freq_ref, consts_ref, w_ref, q_ref, k_ref, v_ref,
                     g_ref):
    i = pl.program_id(1)
    shift = mod_ref[0, 0:1, :]
    scale = mod_ref[0, 1:2, :]
    u = (x_ref[0] * (1.0 + scale) + shift).astype(BF16)
    cos_t, sin_up, sin_dn = _rope_tables(pos_ref, i, freq_ref, consts_ref)
    half = A_ROT // 2
    qk_scale = A_HEAD_DIM ** -0.5 * LOG2E

    def rope_store(col0, o_ref, mult):
        acc = jnp.dot(u, w_ref[:, col0:col0 + A_WIDTH], preferred_element_type=F32)
        for g in range(A_WIDTH // LANES):
            a = acc[:, g * LANES:(g + 1) * LANES]
            r = _rope_apply(a, cos_t, sin_up, sin_dn, half)
            if mult != 1.0:
                r = r * mult
            o_ref[0, :, g * LANES:(g + 1) * LANES] = r.astype(BF16)

    rope_store(0, q_ref, qk_scale)
    rope_store(A_WIDTH, k_ref, 1.0)
    v_ref[0] = jnp.dot(u, w_ref[:, 2 * A_WIDTH:3 * A_WIDTH],
                       preferred_element_type=F32).astype(BF16)
    gate = jnp.dot(u, w_ref[:, 3 * A_WIDTH:4 * A_WIDTH], preferred_element_type=F32)
    g_ref[0] = _silu(gate).astype(BF16)


def _a_inproj(x, mod, pos3, rope, w_bf16):
    b, s, d = x.shape
    freq, consts = rope
    n = w_bf16.shape[1]
    out = jax.ShapeDtypeStruct((b, s, A_WIDTH), BF16)
    ospec = pl.BlockSpec((1, ROW_TILE, A_WIDTH), lambda bi, i: (bi, i, 0))
    return pl.pallas_call(
        _a_inproj_kernel,
        grid=(b, s // ROW_TILE),
        in_specs=[pl.BlockSpec((1, ROW_TILE, d), lambda bi, i: (bi, i, 0)),
                  pl.BlockSpec((1, 3, d), lambda bi, i: (bi, 0, 0)),
                  pl.BlockSpec((1, s // LANES, LANES), lambda bi, i: (bi, 0, 0)),
                  pl.BlockSpec((FREQ_ROWS, LANES), lambda bi, i: (0, 0)),
                  pl.BlockSpec((8, LANES), lambda bi, i: (0, 0)),
                  pl.BlockSpec((d, n), lambda bi, i: (0, 0))],
        out_specs=[ospec, ospec, ospec, ospec],
        out_shape=[out, out, out, out],
        compiler_params=pltpu.CompilerParams(vmem_limit_bytes=VMEM_LIMIT),
        name="a_inproj",
    )(x, mod, pos3, freq, consts, w_bf16)


def _chunk_mask():
    krow = lax.broadcasted_iota(jnp.int32, (KV_TILE, Q_TILE), 0) // CHUNK
    qcol = lax.broadcasted_iota(jnp.int32, (KV_TILE, Q_TILE), 1) // CHUNK
    return krow <= qcol


def _score_items(q, k_rows, kv_len, s_ref, mask, out):
    diag = kv_len - KV_TILE

    def boundary():
        s_d = lax.dot_general(k_rows(diag, kv_len), q, _NT, preferred_element_type=F32)
        s_d = jnp.where(mask, s_d, NEG_INF).astype(SCORE_DTYPE)
        s_ref[diag:kv_len, :] = s_d
        out["m"] = jnp.max(s_d, axis=0, keepdims=True)

    def interior(r0, r1):
        s_o = lax.dot_general(k_rows(r0, r1), q, _NT,
                              preferred_element_type=F32).astype(SCORE_DTYPE)
        s_ref[r0:r1, :] = s_o
        out["m"] = jnp.maximum(out["m"], jnp.max(s_o, axis=0, keepdims=True))

    items = [boundary]
    for r0 in range(0, diag, SCORE_ROWS):
        items.append(functools.partial(interior, r0, min(r0 + SCORE_ROWS, diag)))
    return items


def _exp_items(kv_len, s_ref, p_ref, state):
    def block(r0):
        d = (s_ref[r0:r0 + KV_TILE, :] - state["m"]).astype(F32)
        p = jnp.exp2(d)
        if not ONES_ROWS:
            part = jnp.sum(p, axis=0, keepdims=True)
            state["l"] = part if "l" not in state else state["l"] + part
        p_ref[r0:r0 + KV_TILE, :] = p.astype(BF16)

    return [functools.partial(block, r0) for r0 in range(0, kv_len, KV_TILE)]


def _emit_interleaved(item_lists):
    item_lists = [items for items in item_lists if items]
    done = [0] * len(item_lists)
    for _ in range(sum(len(items) for items in item_lists)):
        i = min((i for i in range(len(item_lists)) if done[i] < len(item_lists[i])),
                key=lambda i: (done[i] + 0.5) / len(item_lists[i]))
        item_lists[i][done[i]]()
        done[i] += 1


def _attention_pipeline(n_q, n_streams, dv, q_of, k_rows_of, vt_cols_of, s_ref, p_ref,
                        finalize):
    mask = _chunk_mask()
    state = {}

    def slot(qi, st):
        return (qi % 2) * n_streams + st

    def pv_item(qi, st):
        kv_len = (qi + 1) * Q_TILE
        state[qi, st]["acc"] = jnp.dot(vt_cols_of(st)(0, kv_len),
                                       p_ref[slot(qi, st), 0:kv_len, :],
                                       preferred_element_type=F32)

    for t in range(n_q + 2):
        scores, exps, pvs = [], [], []
        for st in range(n_streams):
            if t < n_q:
                state[t, st] = {}
                scores.append(_score_items(q_of(t, st), k_rows_of(st), (t + 1) * Q_TILE,
                                           s_ref.at[slot(t, st)], mask, state[t, st]))
            if 0 <= t - 1 < n_q:
                exps.append(_exp_items(t * Q_TILE, s_ref.at[slot(t - 1, st)],
                                       p_ref.at[slot(t - 1, st)], state[t - 1, st]))
            if 0 <= t - 2 < n_q:
                pvs.append([functools.partial(pv_item, t - 2, st)])
        if EMIT_ORDER == "fine":
            _emit_interleaved(scores + exps + pvs)
        elif EMIT_ORDER == "coarse":
            _emit_interleaved(scores)
            _emit_interleaved(exps)
            _emit_interleaved(pvs)
        elif EMIT_ORDER == "mxu_first":
            _emit_interleaved(pvs)
            _emit_interleaved(scores)
            _emit_interleaved(exps)
        if 0 <= t - 2 < n_q:
            done = [state.pop((t - 2, st)) for st in range(n_streams)]
            if ONES_ROWS:
                finalize(t - 2, [(d["acc"][dv:dv + 1, :], d["acc"][0:dv, :]) for d in done])
            else:
                finalize(t - 2, [(d["l"], d["acc"]) for d in done])


def _store_transposed_values(v_ref, vt_ref, dv):
    n_heads = v_ref.shape[2] // dv
    stride = dv + ONES_ROWS
    ones = jnp.ones((ONES_ROWS, KV_TILE), BF16)
    for r0 in range(0, v_ref.shape[1], KV_TILE):
        blk_t = v_ref[0, r0:r0 + KV_TILE, :].astype(F32).T.astype(BF16)
        for h in range(n_heads):
            vt_ref[h * stride:h * stride + dv, r0:r0 + KV_TILE] = blk_t[h * dv:(h + 1) * dv, :]
            if ONES_ROWS:
                vt_ref[h * stride + dv:(h + 1) * stride, r0:r0 + KV_TILE] = ones


def _a_attn_kernel(lambda_init, q_ref, k_ref, v_ref, g_ref, lam_ref, subg_ref, o_ref,
                   kp_ref, vt_ref, s_ref, p_ref):
    s = q_ref.shape[1]
    lane = lax.broadcasted_iota(jnp.int32, (KV_TILE, LANES), 1)
    first_map = lane < A_HEAD_DIM
    for r0 in range(0, s, KV_TILE):
        for hh in range(A_HEADS_PER_STEP):
            kb = k_ref[0, r0:r0 + KV_TILE, hh * LANES:(hh + 1) * LANES].astype(F32)
            kp_ref[2 * hh, r0:r0 + KV_TILE, :] = jnp.where(first_map, kb, 0.0).astype(BF16)
            kp_ref[2 * hh + 1, r0:r0 + KV_TILE, :] = jnp.where(first_map, 0.0, kb).astype(BF16)
    dv = 2 * A_HEAD_DIM
    stride = dv + ONES_ROWS
    _store_transposed_values(v_ref, vt_ref, dv)

    lam = (jnp.exp(jnp.sum(lam_ref[0:1, :] * lam_ref[1:2, :], axis=1, keepdims=True))
           - jnp.exp(jnp.sum(lam_ref[2:3, :] * lam_ref[3:4, :], axis=1, keepdims=True))
           + lambda_init)
    out_gain = subg_ref[0:1, :] * (1.0 - lambda_init)

    def finalize(qi, outs):
        rows = slice(qi * Q_TILE, (qi + 1) * Q_TILE)
        for hh in range(A_HEADS_PER_STEP):
            cols = slice(hh * LANES, (hh + 1) * LANES)
            (l1, acc1), (l2, acc2) = outs[2 * hh:2 * hh + 2]
            o_t = acc1 * (1.0 / l1) - acc2 * (lam / l2)
            ms = jnp.mean(o_t * o_t, axis=0, keepdims=True)
            o_t = o_t * lax.rsqrt(ms + SUBLN_EPS)
            o = o_t.T * out_gain * g_ref[0, rows, cols].astype(F32)
            o_ref[0, rows, cols] = o.astype(BF16)

    _attention_pipeline(
        s // Q_TILE, 2 * A_HEADS_PER_STEP, dv,
        lambda qi, st: q_ref[0, qi * Q_TILE:(qi + 1) * Q_TILE,
                             (st // 2) * LANES:(st // 2 + 1) * LANES],
        lambda st: (lambda r0, r1: kp_ref[st, r0:r1, :]),
        lambda st: (lambda c0, c1: vt_ref[(st // 2) * stride:(st // 2 + 1) * stride, c0:c1]),
        s_ref, p_ref, finalize)


def _a_attention(q, k, v, g, lam_rows, subg_rows, lambda_init):
    b, s, _ = q.shape
    n_streams = 2 * A_HEADS_PER_STEP
    spec = pl.BlockSpec((1, s, A_HEADS_PER_STEP * LANES), lambda bi, h: (bi, 0, h))
    small = pl.BlockSpec((8, LANES), lambda bi, h: (0, 0))
    return pl.pallas_call(
        functools.partial(_a_attn_kernel, lambda_init),
        grid=(b, A_HEADS // A_HEADS_PER_STEP),
        in_specs=[spec, spec, spec, spec, small, small],
        out_specs=spec,
        out_shape=jax.ShapeDtypeStruct((b, s, A_WIDTH), BF16),
        scratch_shapes=[pltpu.VMEM((n_streams, s, LANES), BF16),
                        pltpu.VMEM((A_HEADS_PER_STEP * (LANES + ONES_ROWS), s), BF16),
                        pltpu.VMEM((2 * n_streams, s, Q_TILE), SCORE_DTYPE),
                        pltpu.VMEM((2 * n_streams, s, Q_TILE), BF16)],
        compiler_params=pltpu.CompilerParams(vmem_limit_bytes=VMEM_LIMIT, flags=ATTN_FLAGS),
        name="a_attention",
    )(q, k, v, g, lam_rows, subg_rows)


def _deepnorm_update(o_ref, w_ref, x_ref, mod_ref, lng_ref, lnb_ref):
    y = jnp.dot(o_ref[0], w_ref[...], preferred_element_type=F32)
    gate = mod_ref[0, 2:3, :]
    z = DEEPNORM_ALPHA * x_ref[0] + gate * y
    mu = jnp.mean(z, axis=1, keepdims=True)
    zc = z - mu
    var = jnp.mean(zc * zc, axis=1, keepdims=True)
    return zc * lax.rsqrt(var + LN_EPS) * lng_ref[...] + lnb_ref[...]


def _outproj_ln_kernel(o_ref, w_ref, x_ref, mod_ref, lng_ref, lnb_ref, y_ref):
    y_ref[0] = _deepnorm_update(o_ref, w_ref, x_ref, mod_ref, lng_ref, lnb_ref)


def _outproj_ln(o, w_bf16, x, mod, ln_g, ln_b):
    b, s, d = x.shape
    width = o.shape[2]
    row = pl.BlockSpec((1, d), lambda bi, i: (0, 0))
    return pl.pallas_call(
        _outproj_ln_kernel,
        grid=(b, s // ROW_TILE),
        in_specs=[pl.BlockSpec((1, ROW_TILE, width), lambda bi, i: (bi, i, 0)),
                  pl.BlockSpec((width, d), lambda bi, i: (0, 0)),
                  pl.BlockSpec((1, ROW_TILE, d), lambda bi, i: (bi, i, 0)),
                  pl.BlockSpec((1, 3, d), lambda bi, i: (bi, 0, 0)),
                  row, row],
        out_specs=pl.BlockSpec((1, ROW_TILE, d), lambda bi, i: (bi, i, 0)),
        out_shape=jax.ShapeDtypeStruct((b, s, d), F32),
        compiler_params=pltpu.CompilerParams(vmem_limit_bytes=VMEM_LIMIT),
        name="outproj_ln",
    )(o, w_bf16, x, mod, ln_g.reshape(1, d), ln_b.reshape(1, d))


B_HEAD_PAD = LANES
B_QK_PAD = B_HEADS * B_HEAD_PAD
B_COL_KV = B_Q_RANK
B_COL_ROPE = B_Q_RANK + B_KV_RANK
B_COL_GATE = B_COL_ROPE + LANES
B_IN_PAD = B_COL_GATE + B_WIDTH
B_COL_CHUNK = 512


def _rms(x, g_row, eps):
    ms = jnp.mean(x * x, axis=1, keepdims=True)
    return x * lax.rsqrt(ms + eps) * g_row


def _b_inproj_kernel(o_ref, wo_ref, x0_ref, mod0_ref, lng_ref, lnb_ref, mod_ref, pos_ref, freq_ref,
                     consts_ref, w1_ref, qg_ref, wuq_ref, kvg_ref, wk_ref, wv_ref,
                     x_ref, q_ref, k_ref, v_ref, g_ref):
    i = pl.program_id(1)
    x = _deepnorm_update(o_ref, wo_ref, x0_ref, mod0_ref, lng_ref, lnb_ref)
    x_ref[0] = x
    shift = mod_ref[0, 0:1, :]
    scale = mod_ref[0, 1:2, :]
    u = (x * (1.0 + scale) + shift).astype(BF16)
    cos_t, sin_up, sin_dn = _rope_tables(pos_ref, i, freq_ref, consts_ref)
    half = B_ROPE // 2
    qk_scale = (B_NOPE + B_ROPE) ** -0.5 * LOG2E

    lat = jnp.dot(u, w1_ref[:, 0:B_COL_GATE], preferred_element_type=F32)
    qn = _rms(lat[:, 0:B_Q_RANK], qg_ref[...], RMS_EPS).astype(BF16)
    kvn = _rms(lat[:, B_COL_KV:B_COL_ROPE], kvg_ref[...], RMS_EPS).astype(BF16)
    k_rope = _rope_apply(lat[:, B_COL_ROPE:B_COL_GATE], cos_t, sin_up, sin_dn, half)

    cos_q, up_q, dn_q = cos_t * qk_scale, sin_up * qk_scale, sin_dn * qk_scale
    for c0 in range(0, B_QK_PAD, B_COL_CHUNK):
        qc = jnp.dot(qn, wuq_ref[:, c0:c0 + B_COL_CHUNK], preferred_element_type=F32)
        kc = jnp.dot(kvn, wk_ref[:, c0:c0 + B_COL_CHUNK], preferred_element_type=F32)
        for g0 in range(0, B_COL_CHUNK, B_HEAD_PAD):
            cols = slice(c0 + g0, c0 + g0 + B_HEAD_PAD)
            q_ref[0, :, cols] = _rope_apply(qc[:, g0:g0 + B_HEAD_PAD], cos_q, up_q, dn_q,
                                            half).astype(BF16)
            k_ref[0, :, cols] = (kc[:, g0:g0 + B_HEAD_PAD] + k_rope).astype(BF16)
    v_ref[0] = jnp.dot(kvn, wv_ref[...], preferred_element_type=F32).astype(BF16)
    gate = jnp.dot(u, w1_ref[:, B_COL_GATE:B_IN_PAD], preferred_element_type=F32)
    g_ref[0] = _silu(gate).astype(BF16)


def _b_inproj(o, wo, x0, mod0, ln_g, ln_b, mod, pos3, rope, w1, qg, wuq, kvg, wk, wv):
    b, s, d = x0.shape
    freq, consts = rope

    def full(a):
        return pl.BlockSpec(a.shape, lambda bi, i: (0,) * a.ndim,
                            pipeline_mode=pl.Buffered(1))

    def rows(width):
        return pl.BlockSpec((1, ROW_TILE, width), lambda bi, i: (bi, i, 0))

    mod_spec = pl.BlockSpec((1, 3, d), lambda bi, i: (bi, 0, 0))
    ln_g, ln_b = ln_g.reshape(1, d), ln_b.reshape(1, d)
    return pl.pallas_call(
        _b_inproj_kernel,
        grid=(b, s // ROW_TILE),
        in_specs=[rows(o.shape[2]), full(wo), rows(d), mod_spec, full(ln_g), full(ln_b),
                  mod_spec,
                  pl.BlockSpec((1, s // LANES, LANES), lambda bi, i: (bi, 0, 0)),
                  full(freq), full(consts), full(w1), full(qg), full(wuq), full(kvg), full(wk),
                  full(wv)],
        out_specs=[rows(d), rows(B_QK_PAD), rows(B_QK_PAD), rows(B_WIDTH), rows(B_WIDTH)],
        out_shape=[jax.ShapeDtypeStruct((b, s, d), F32),
                   jax.ShapeDtypeStruct((b, s, B_QK_PAD), BF16),
                   jax.ShapeDtypeStruct((b, s, B_QK_PAD), BF16),
                   jax.ShapeDtypeStruct((b, s, B_WIDTH), BF16),
                   jax.ShapeDtypeStruct((b, s, B_WIDTH), BF16)],
        compiler_params=pltpu.CompilerParams(vmem_limit_bytes=VMEM_LIMIT),
        name="b_inproj",
    )(o, wo, x0, mod0, ln_g, ln_b, mod, pos3, freq, consts, w1, qg, wuq, kvg, wk, wv)


B_HEADS_PER_STEP = 4


def _b_attn_kernel(q_ref, k_ref, v_ref, g_ref, o_ref, vt_ref, s_ref, p_ref):
    s = q_ref.shape[1]
    _store_transposed_values(v_ref, vt_ref, B_VDIM)
    stride = B_VDIM + ONES_ROWS

    def finalize(qi, outs):
        rows = slice(qi * Q_TILE, (qi + 1) * Q_TILE)
        o_t = jnp.concatenate([acc * (1.0 / l) for l, acc in outs], axis=0)
        o_ref[0, rows, :] = (o_t.T * g_ref[0, rows, :].astype(F32)).astype(BF16)

    _attention_pipeline(
        s // Q_TILE, B_HEADS_PER_STEP, B_VDIM,
        lambda qi, st: q_ref[0, qi * Q_TILE:(qi + 1) * Q_TILE,
                             st * B_HEAD_PAD:(st + 1) * B_HEAD_PAD],
        lambda st: (lambda r0, r1: k_ref[0, r0:r1, st * B_HEAD_PAD:(st + 1) * B_HEAD_PAD]),
        lambda st: (lambda c0, c1: vt_ref[st * stride:(st + 1) * stride, c0:c1]),
        s_ref, p_ref, finalize)


def _b_attention(q, k, v, g):
    b, s, _ = q.shape
    qk_spec = pl.BlockSpec((1, s, B_HEADS_PER_STEP * B_HEAD_PAD), lambda bi, h: (bi, 0, h))
    spec = pl.BlockSpec((1, s, B_HEADS_PER_STEP * B_VDIM), lambda bi, h: (bi, 0, h))
    return pl.pallas_call(
        _b_attn_kernel,
        grid=(b, B_HEADS // B_HEADS_PER_STEP),
        in_specs=[qk_spec, qk_spec, spec, spec],
        out_specs=spec,
        out_shape=jax.ShapeDtypeStruct((b, s, B_WIDTH), BF16),
        scratch_shapes=[pltpu.VMEM((B_HEADS_PER_STEP * (B_VDIM + ONES_ROWS), s), BF16),
                        pltpu.VMEM((2 * B_HEADS_PER_STEP, s, Q_TILE), SCORE_DTYPE),
                        pltpu.VMEM((2 * B_HEADS_PER_STEP, s, Q_TILE), BF16)],
        compiler_params=pltpu.CompilerParams(vmem_limit_bytes=VMEM_LIMIT, flags=ATTN_FLAGS),
        name="b_attention",
    )(q, k, v, g)


def _pad_rows8(rows):
    out = jnp.zeros((8, LANES), F32)
    for r, vec in enumerate(rows):
        out = out.at[r, :vec.shape[0]].set(vec.astype(F32))
    return out


def _b_layouts(w_in, w_uq, w_ukv):
    d = w_in.shape[0]
    rope_cols = jnp.zeros((d, LANES), F32).at[:, B_NOPE:B_NOPE + B_ROPE].set(
        w_in[:, B_COL_ROPE:B_COL_ROPE + B_ROPE])
    w1 = jnp.concatenate([w_in[:, :B_COL_ROPE], rope_cols, w_in[:, B_COL_ROPE + B_ROPE:]], axis=1)
    uq = w_uq.reshape(B_Q_RANK, B_HEADS, B_NOPE + B_ROPE)
    uq = jnp.pad(uq, ((0, 0), (0, 0), (0, B_HEAD_PAD - B_NOPE - B_ROPE)))
    ukv = w_ukv.reshape(B_KV_RANK, B_HEADS, B_NOPE + B_VDIM)
    wk = jnp.pad(ukv[:, :, :B_NOPE], ((0, 0), (0, 0), (0, B_HEAD_PAD - B_NOPE)))
    wv = ukv[:, :, B_NOPE:]
    return (w1.astype(BF16), uq.reshape(B_Q_RANK, B_QK_PAD).astype(BF16),
            wk.reshape(B_KV_RANK, B_QK_PAD).astype(BF16),
            wv.reshape(B_KV_RANK, B_WIDTH).astype(BF16))


def kernel(x, c, positions, ada_w, ada_b, ln_g, ln_b, a_w_in, a_lambda_q1, a_lambda_k1,
           a_lambda_q2, a_lambda_k2, a_subln_g, a_w_out, b_w_in, b_q_norm_g, b_w_uq,
           b_kv_norm_g, b_w_ukv, b_w_out):
    b, s, d = x.shape
    assert d == D_MODEL and s % ROW_TILE == 0 and s % Q_TILE == 0 and Q_TILE == KV_TILE
    mod = _modulation(c, ada_w, ada_b).reshape(DEPTH, b, 3, d)
    pos3 = positions.reshape(b, s // LANES, LANES)

    lambda_init = 0.8 - 0.6 * math.exp(-0.3 * 0)
    q, k, v, g = _a_inproj(x, mod[0], pos3, _rope_consts(A_ROT, A_HEAD_DIM, 0),
                           a_w_in[0].astype(BF16))
    lam_rows = _pad_rows8([a_lambda_q1[0], a_lambda_k1[0], a_lambda_q2[0], a_lambda_k2[0]])
    o = _a_attention(q, k, v, g, lam_rows, _pad_rows8([a_subln_g[0]]), lambda_init)

    w1, wuq, wk, wv = _b_layouts(b_w_in[0], b_w_uq[0], b_w_ukv[0])
    x, q, k, v, g = _b_inproj(o, a_w_out[0].astype(BF16), x, mod[0], ln_g[0], ln_b[0], mod[1],
                              pos3, _rope_consts(B_ROPE, B_HEAD_PAD, B_NOPE), w1,
                              b_q_norm_g[0].reshape(1, B_Q_RANK), wuq,
                              b_kv_norm_g[0].reshape(1, B_KV_RANK), wk, wv)
    o = _b_attention(q, k, v, g)
    return _outproj_ln(o, b_w_out[0].astype(BF16), x, mod[1], ln_g[1], ln_b[1])
```

```python
import functools
import math

import jax
import jax.numpy as jnp
import numpy as np
from jax import lax
from jax.experimental import pallas as pl
from jax.experimental.pallas import tpu as pltpu

F32 = jnp.float32
BF16 = jnp.bfloat16

D_MODEL = 1024
CHUNK = 64
ROPE_THETA = 500000.0

A_HEADS = 8
A_HEAD_DIM = 64
A_ROT = A_HEAD_DIM // 4
A_WIDTH = A_HEADS * 2 * A_HEAD_DIM

B_HEADS = 16
B_NOPE = 64
B_ROPE = 32
B_VDIM = 64
B_Q_RANK = 512
B_KV_RANK = 256
B_WIDTH = B_HEADS * B_VDIM

DEPTH = 2
DEEPNORM_ALPHA = (2.0 * DEPTH) ** 0.25
LN_EPS = 1e-5
RMS_EPS = 1e-6
SUBLN_EPS = 1e-5
NEG_INF = -1e30

LANES = 128
FREQ_ROWS = 16
ROW_TILE = 512
OUT_ROW_TILE = 1024
Q_TILE = 256
KV_TILE = 256
SCORE_ROWS = 512
A_HEADS_PER_STEP = 2
LOG2E = math.log2(math.e)
VMEM_LIMIT = 56 * 1024 * 1024

_NT = (((1,), (1,)), ((), ()))


def _silu(x):
    return x * (1.0 / (1.0 + jnp.exp(-x)))


def _mod_kernel(c_ref, w_ref, b_ref, o_ref):
    ca = _silu(c_ref[...]).astype(BF16)
    o_ref[0] = jnp.dot(ca, w_ref[0].astype(BF16), preferred_element_type=F32) + b_ref[0]


def _modulation(c, ada_w, ada_b):
    depth, d, n3 = ada_w.shape
    b = c.shape[0]
    tn = 1024
    return pl.pallas_call(
        _mod_kernel,
        grid=(depth, n3 // tn),
        in_specs=[pl.BlockSpec((b, d), lambda i, j: (0, 0)),
                  pl.BlockSpec((1, d, tn), lambda i, j: (i, 0, j)),
                  pl.BlockSpec((1, 1, tn), lambda i, j: (i, 0, j))],
        out_specs=pl.BlockSpec((1, b, tn), lambda i, j: (i, 0, j)),
        out_shape=jax.ShapeDtypeStruct((depth, b, n3), F32),
        compiler_params=pltpu.CompilerParams(vmem_limit_bytes=VMEM_LIMIT),
        name="modulation",
    )(c, ada_w, ada_b.reshape(depth, 1, n3))


def _rope_tables(pos_ref, first_chunk, n_chunks, freq_ref, consts_ref):
    freq = freq_ref[...]
    cs, sn = [], []
    for r in range(n_chunks):
        pos = pos_ref[0, pl.ds(first_chunk + r, 1), :].astype(F32)
        ang = freq * pos
        reps = LANES // FREQ_ROWS
        cs.append(jnp.concatenate([jnp.cos(ang)] * reps, axis=0).T)
        sn.append(jnp.concatenate([jnp.sin(ang)] * reps, axis=0).T)
    cos_t = jnp.where(consts_ref[0:1, :] > 0.0, jnp.concatenate(cs, axis=0), 1.0)
    sin_t = jnp.concatenate(sn, axis=0)
    return cos_t, sin_t * consts_ref[1:2, :], sin_t * consts_ref[2:3, :]


def _rope_apply(a, cos_t, sin_up, sin_dn, half):
    return (a * cos_t + pltpu.roll(a, LANES - half, 1) * sin_up
            + pltpu.roll(a, half, 1) * sin_dn)


def _rope_consts(rot_dim, group, offset):
    half = rot_dim // 2
    assert FREQ_ROWS % half == 0 and offset % half == 0 and group % half == 0
    inv_freq = ROPE_THETA ** (-jnp.arange(0, rot_dim, 2, dtype=F32) / rot_dim)
    freq_rows = jnp.broadcast_to(jnp.tile(inv_freq, FREQ_ROWS // half)[:, None],
                                 (FREQ_ROWS, LANES))
    d = np.arange(LANES) % group - offset
    rotated = (d >= 0) & (d < rot_dim)
    first = rotated & (d < half)
    second = rotated & (d >= half)
    consts = np.zeros((8, LANES), np.float32)
    consts[0], consts[1], consts[2] = rotated, -first.astype(np.float32), second
    return freq_rows, jnp.asarray(consts)


def _a_inproj_kernel(x_ref, mod_ref, pos_ref, freq_ref, consts_ref, w_ref, q_ref, k_ref, v_ref,
                     g_ref):
    i = pl.program_id(1)
    shift = mod_ref[0, 0:1, :]
    scale = mod_ref[0, 1:2, :]
    u = (x_ref[0] * (1.0 + scale) + shift).astype(BF16)
    cos_t, sin_up, sin_dn = _rope_tables(pos_ref, i * (ROW_TILE // LANES), ROW_TILE // LANES,
                                         freq_ref, consts_ref)
    half = A_ROT // 2
    qk_scale = A_HEAD_DIM ** -0.5 * LOG2E

    def rope_store(col0, o_ref, mult):
        acc = jnp.dot(u, w_ref[:, col0:col0 + A_WIDTH], preferred_element_type=F32)
        for g in range(A_WIDTH // LANES):
            a = acc[:, g * LANES:(g + 1) * LANES]
            r = _rope_apply(a, cos_t, sin_up, sin_dn, half)
            if mult != 1.0:
                r = r * mult
            o_ref[0, :, g * LANES:(g + 1) * LANES] = r.astype(BF16)

    rope_store(0, q_ref, qk_scale)
    rope_store(A_WIDTH, k_ref, 1.0)
    v_ref[0] = jnp.dot(u, w_ref[:, 2 * A_WIDTH:3 * A_WIDTH],
                       preferred_element_type=F32).astype(BF16)
    gate = jnp.dot(u, w_ref[:, 3 * A_WIDTH:4 * A_WIDTH], preferred_element_type=F32)
    g_ref[0] = _silu(gate).astype(BF16)


def _a_inproj(x, mod, pos3, rope, w_bf16):
    b, s, d = x.shape
    freq, consts = rope
    n = w_bf16.shape[1]
    out = jax.ShapeDtypeStruct((b, s, A_WIDTH), BF16)
    ospec = pl.BlockSpec((1, ROW_TILE, A_WIDTH), lambda bi, i: (bi, i, 0))
    return pl.pallas_call(
        _a_inproj_kernel,
        grid=(b, s // ROW_TILE),
        in_specs=[pl.BlockSpec((1, ROW_TILE, d), lambda bi, i: (bi, i, 0)),
                  pl.BlockSpec((1, 3, d), lambda bi, i: (bi, 0, 0)),
                  pl.BlockSpec((1, s // LANES, LANES), lambda bi, i: (bi, 0, 0)),
                  pl.BlockSpec((FREQ_ROWS, LANES), lambda bi, i: (0, 0)),
                  pl.BlockSpec((8, LANES), lambda bi, i: (0, 0)),
                  pl.BlockSpec((d, n), lambda bi, i: (0, 0))],
        out_specs=[ospec, ospec, ospec, ospec],
        out_shape=[out, out, out, out],
        compiler_params=pltpu.CompilerParams(vmem_limit_bytes=VMEM_LIMIT),
        name="a_inproj",
    )(x, mod, pos3, freq, consts, w_bf16)


def _chunk_mask():
    krow = lax.broadcasted_iota(jnp.int32, (KV_TILE, Q_TILE), 0) // CHUNK
    qcol = lax.broadcasted_iota(jnp.int32, (KV_TILE, Q_TILE), 1) // CHUNK
    return krow <= qcol


def _score_items(q, k_rows, kv_len, s_ref, mask, out):
    diag = kv_len - KV_TILE

    def boundary():
        s_d = lax.dot_general(k_rows(diag, kv_len), q, _NT, preferred_element_type=F32)
        s_d = jnp.where(mask, s_d, NEG_INF)
        s_ref[diag:kv_len, :] = s_d
        out["m"] = jnp.max(s_d, axis=0, keepdims=True)

    def interior(r0, r1):
        s_o = lax.dot_general(k_rows(r0, r1), q, _NT, preferred_element_type=F32)
        s_ref[r0:r1, :] = s_o
        out["m"] = jnp.maximum(out["m"], jnp.max(s_o, axis=0, keepdims=True))

    items = [boundary]
    for r0 in range(0, diag, SCORE_ROWS):
        items.append(functools.partial(interior, r0, min(r0 + SCORE_ROWS, diag)))
    return items


def _exp_items(kv_len, s_ref, p_ref, state):
    def block(r0):
        p = jnp.exp2(s_ref[r0:r0 + KV_TILE, :] - state["m"])
        part = jnp.sum(p, axis=0, keepdims=True)
        state["l"] = part if "l" not in state else state["l"] + part
        p_ref[r0:r0 + KV_TILE, :] = p.astype(BF16)

    return [functools.partial(block, r0) for r0 in range(0, kv_len, KV_TILE)]


def _emit_interleaved(item_lists):
    item_lists = [items for items in item_lists if items]
    done = [0] * len(item_lists)
    for _ in range(sum(len(items) for items in item_lists)):
        i = min((i for i in range(len(item_lists)) if done[i] < len(item_lists[i])),
                key=lambda i: (done[i] + 0.5) / len(item_lists[i]))
        item_lists[i][done[i]]()
        done[i] += 1


def _attention_pipeline(n_q, n_streams, q_of, k_rows_of, vt_cols_of, s_ref, p_ref, finalize):
    mask = _chunk_mask()
    state = {}

    def slot(qi, st):
        return (qi % 2) * n_streams + st

    def pv_item(qi, st):
        kv_len = (qi + 1) * Q_TILE
        state[qi, st]["acc"] = jnp.dot(vt_cols_of(st)(0, kv_len),
                                       p_ref[slot(qi, st), 0:kv_len, :],
                                       preferred_element_type=F32)

    order = list(range(n_q))
    for t in range(n_q + 2):
        scores, exps, pvs = [], [], []
        for st in range(n_streams):
            if t < n_q:
                qi = order[t]
                state[qi, st] = {}
                scores.append(_score_items(q_of(qi, st), k_rows_of(st), (qi + 1) * Q_TILE,
                                           s_ref.at[slot(qi, st)], mask, state[qi, st]))
            if 0 <= t - 1 < n_q:
                qi = order[t - 1]
                exps.append(_exp_items((qi + 1) * Q_TILE, s_ref.at[slot(qi, st)],
                                       p_ref.at[slot(qi, st)], state[qi, st]))
            if 0 <= t - 2 < n_q:
                pvs.append([functools.partial(pv_item, order[t - 2], st)])
        _emit_interleaved(scores + exps + pvs)
        if 0 <= t - 2 < n_q:
            qi = order[t - 2]
            done = [state.pop((qi, st)) for st in range(n_streams)]
            finalize(qi, [(d["l"], d["acc"]) for d in done])


def _store_transposed_values(v_ref, vt_ref):
    for r0 in range(0, v_ref.shape[1], KV_TILE):
        blk = v_ref[0, r0:r0 + KV_TILE, :].astype(F32)
        vt_ref[:, r0:r0 + KV_TILE] = blk.T.astype(BF16)


def _a_attn_kernel(lambda_init, q_ref, k_ref, v_ref, g_ref, lam_ref, subg_ref, o_ref,
                   kp_ref, vt_ref, s_ref, p_ref):
    s = q_ref.shape[1]
    lane = lax.broadcasted_iota(jnp.int32, (KV_TILE, LANES), 1)
    first_map = lane < A_HEAD_DIM
    for r0 in range(0, s, KV_TILE):
        for hh in range(A_HEADS_PER_STEP):
            kb = k_ref[0, r0:r0 + KV_TILE, hh * LANES:(hh + 1) * LANES].astype(F32)
            kp_ref[2 * hh, r0:r0 + KV_TILE, :] = jnp.where(first_map, kb, 0.0).astype(BF16)
            kp_ref[2 * hh + 1, r0:r0 + KV_TILE, :] = jnp.where(first_map, 0.0, kb).astype(BF16)
    dv = 2 * A_HEAD_DIM
    _store_transposed_values(v_ref, vt_ref)

    lam = (jnp.exp(jnp.sum(lam_ref[0:1, :] * lam_ref[1:2, :], axis=1, keepdims=True))
           - jnp.exp(jnp.sum(lam_ref[2:3, :] * lam_ref[3:4, :], axis=1, keepdims=True))
           + lambda_init)
    out_gain = subg_ref[0:1, :] * (1.0 - lambda_init)

    def finalize(qi, outs):
        rows = slice(qi * Q_TILE, (qi + 1) * Q_TILE)
        for hh in range(A_HEADS_PER_STEP):
            cols = slice(hh * LANES, (hh + 1) * LANES)
            (l1, acc1), (l2, acc2) = outs[2 * hh:2 * hh + 2]
            o_t = acc1 * (1.0 / l1) - acc2 * (lam / l2)
            ms = jnp.mean(o_t * o_t, axis=0, keepdims=True)
            o_t = o_t * lax.rsqrt(ms + SUBLN_EPS)
            o = o_t.T * out_gain * g_ref[0, rows, cols].astype(F32)
            o_ref[0, rows, cols] = o.astype(BF16)

    _attention_pipeline(
        s // Q_TILE, 2 * A_HEADS_PER_STEP,
        lambda qi, st: q_ref[0, qi * Q_TILE:(qi + 1) * Q_TILE,
                             (st // 2) * LANES:(st // 2 + 1) * LANES],
        lambda st: (lambda r0, r1: kp_ref[st, r0:r1, :]),
        lambda st: (lambda c0, c1: vt_ref[(st // 2) * dv:(st // 2 + 1) * dv, c0:c1]),
        s_ref, p_ref, finalize)


def _a_attention(q, k, v, g, lam_rows, subg_rows, lambda_init):
    b, s, _ = q.shape
    n_streams = 2 * A_HEADS_PER_STEP
    spec = pl.BlockSpec((1, s, A_HEADS_PER_STEP * LANES), lambda bi, h: (bi, 0, h))
    small = pl.BlockSpec((8, LANES), lambda bi, h: (0, 0))
    return pl.pallas_call(
        functools.partial(_a_attn_kernel, lambda_init),
        grid=(b, A_HEADS // A_HEADS_PER_STEP),
        in_specs=[spec, spec, spec, spec, small, small],
        out_specs=spec,
        out_shape=jax.ShapeDtypeStruct((b, s, A_WIDTH), BF16),
        scratch_shapes=[pltpu.VMEM((n_streams, s, LANES), BF16),
                        pltpu.VMEM((A_HEADS_PER_STEP * LANES, s), BF16),
                        pltpu.VMEM((2 * n_streams, s, Q_TILE), F32),
                        pltpu.VMEM((2 * n_streams, s, Q_TILE), BF16)],
        compiler_params=pltpu.CompilerParams(vmem_limit_bytes=VMEM_LIMIT),
        name="a_attention",
    )(q, k, v, g, lam_rows, subg_rows)


def _deepnorm(y, x, mod_ref, lng_ref, lnb_ref):
    gate = mod_ref[0, 2:3, :]
    z = DEEPNORM_ALPHA * x + gate * y
    mu = jnp.mean(z, axis=1, keepdims=True)
    zc = z - mu
    var = jnp.mean(zc * zc, axis=1, keepdims=True)
    return zc * lax.rsqrt(var + LN_EPS) * lng_ref[...] + lnb_ref[...]


def _outproj_ln_kernel(o_ref, w_ref, x_ref, mod_ref, lng_ref, lnb_ref, y_ref):
    y = jnp.dot(o_ref[0], w_ref[...], preferred_element_type=F32)
    y_ref[0] = _deepnorm(y, x_ref[0], mod_ref, lng_ref, lnb_ref)


def _outproj_ln(o, w_bf16, x, mod, ln_g, ln_b):
    b, s, d = x.shape
    width = o.shape[2]
    row = pl.BlockSpec((1, d), lambda bi, i: (0, 0))
    return pl.pallas_call(
        _outproj_ln_kernel,
        grid=(b, s // OUT_ROW_TILE),
        in_specs=[pl.BlockSpec((1, OUT_ROW_TILE, width), lambda bi, i: (bi, i, 0)),
                  pl.BlockSpec((width, d), lambda bi, i: (0, 0)),
                  pl.BlockSpec((1, OUT_ROW_TILE, d), lambda bi, i: (bi, i, 0)),
                  pl.BlockSpec((1, 3, d), lambda bi, i: (bi, 0, 0)),
                  row, row],
        out_specs=pl.BlockSpec((1, OUT_ROW_TILE, d), lambda bi, i: (bi, i, 0)),
        out_shape=jax.ShapeDtypeStruct((b, s, d), F32),
        compiler_params=pltpu.CompilerParams(vmem_limit_bytes=VMEM_LIMIT),
        name="outproj_ln",
    )(o, w_bf16, x, mod, ln_g.reshape(1, d), ln_b.reshape(1, d))


B_HEAD_PAD = LANES
B_QK_PAD = B_HEADS * B_HEAD_PAD
B_COL_KV = B_Q_RANK
B_COL_ROPE = B_Q_RANK + B_KV_RANK
B_COL_GATE = B_COL_ROPE + LANES
B_IN_PAD = B_COL_GATE + B_WIDTH
B_COL_CHUNK = 512


def _rms(x, g_row, eps):
    ms = jnp.mean(x * x, axis=1, keepdims=True)
    return x * lax.rsqrt(ms + eps) * g_row


def _b_inproj_kernel(o_ref, wo_ref, x0_ref, mod0_ref, lng_ref, lnb_ref, mod_ref, pos_ref, freq_ref,
                     consts_ref, w1_ref, qg_ref, wuq_ref, kvg_ref, wk_ref, wv_ref,
                     x_ref, q_ref, k_ref, v_ref, g_ref):
    i = pl.program_id(1)
    shift = mod_ref[0, 0:1, :]
    scale = mod_ref[0, 1:2, :]
    y = jnp.dot(o_ref[0], wo_ref[...], preferred_element_type=F32)
    x = _deepnorm(y, x0_ref[0], mod0_ref, lng_ref, lnb_ref)
    x_ref[0] = x
    u = (x * (1.0 + scale) + shift).astype(BF16)
    cos_t, sin_up, sin_dn = _rope_tables(pos_ref, i * (ROW_TILE // LANES), ROW_TILE // LANES,
                                         freq_ref, consts_ref)
    half = B_ROPE // 2
    qk_scale = (B_NOPE + B_ROPE) ** -0.5 * LOG2E

    lat = jnp.dot(u, w1_ref[:, 0:B_COL_GATE], preferred_element_type=F32)
    qn = _rms(lat[:, 0:B_Q_RANK], qg_ref[...], RMS_EPS).astype(BF16)
    kvn = _rms(lat[:, B_COL_KV:B_COL_ROPE], kvg_ref[...], RMS_EPS).astype(BF16)
    k_rope = _rope_apply(lat[:, B_COL_ROPE:B_COL_GATE], cos_t, sin_up, sin_dn, half)

    cos_q, up_q, dn_q = cos_t * qk_scale, sin_up * qk_scale, sin_dn * qk_scale
    for c0 in range(0, B_QK_PAD, B_COL_CHUNK):
        qc = jnp.dot(qn, wuq_ref[:, c0:c0 + B_COL_CHUNK], preferred_element_type=F32)
        kc = jnp.dot(kvn, wk_ref[:, c0:c0 + B_COL_CHUNK], preferred_element_type=F32)
        for g0 in range(0, B_COL_CHUNK, B_HEAD_PAD):
            cols = slice(c0 + g0, c0 + g0 + B_HEAD_PAD)
            q_ref[0, :, cols] = _rope_apply(qc[:, g0:g0 + B_HEAD_PAD], cos_q, up_q, dn_q,
                                            half).astype(BF16)
            k_ref[0, :, cols] = (kc[:, g0:g0 + B_HEAD_PAD] + k_rope).astype(BF16)
    v_ref[0] = jnp.dot(kvn, wv_ref[...], preferred_element_type=F32).astype(BF16)
    gate = jnp.dot(u, w1_ref[:, B_COL_GATE:B_IN_PAD], preferred_element_type=F32)
    g_ref[0] = _silu(gate).astype(BF16)


def _b_inproj(o, wo, x0, mod0, ln_g, ln_b, mod, pos3, rope, w1, qg, wuq, kvg, wk, wv):
    b, s, d = x0.shape
    freq, consts = rope

    def full(a):
        return pl.BlockSpec(a.shape, lambda bi, i: (0,) * a.ndim,
                            pipeline_mode=pl.Buffered(1))

    def rows(width):
        return pl.BlockSpec((1, ROW_TILE, width), lambda bi, i: (bi, i, 0))

    mod_spec = pl.BlockSpec((1, 3, d), lambda bi, i: (bi, 0, 0))
    ln_g, ln_b = ln_g.reshape(1, d), ln_b.reshape(1, d)
    return pl.pallas_call(
        _b_inproj_kernel,
        grid=(b, s // ROW_TILE),
        in_specs=[rows(o.shape[2]), full(wo), rows(d), mod_spec, full(ln_g), full(ln_b),
                  mod_spec,
                  pl.BlockSpec((1, s // LANES, LANES), lambda bi, i: (bi, 0, 0)),
                  full(freq), full(consts), full(w1), full(qg), full(wuq), full(kvg), full(wk),
                  full(wv)],
        out_specs=[rows(d), rows(B_QK_PAD), rows(B_QK_PAD), rows(B_WIDTH), rows(B_WIDTH)],
        out_shape=[jax.ShapeDtypeStruct((b, s, d), F32),
                   jax.ShapeDtypeStruct((b, s, B_QK_PAD), BF16),
                   jax.ShapeDtypeStruct((b, s, B_QK_PAD), BF16),
                   jax.ShapeDtypeStruct((b, s, B_WIDTH), BF16),
                   jax.ShapeDtypeStruct((b, s, B_WIDTH), BF16)],
        compiler_params=pltpu.CompilerParams(vmem_limit_bytes=VMEM_LIMIT),
        name="b_inproj",
    )(o, wo, x0, mod0, ln_g, ln_b, mod, pos3, freq, consts, w1, qg, wuq, kvg, wk, wv)


B_HEADS_PER_STEP = 4


def _b_attn_kernel(q_ref, k_ref, v_ref, g_ref, o_ref, vt_ref, s_ref, p_ref):
    s = q_ref.shape[1]
    _store_transposed_values(v_ref, vt_ref)

    def finalize(qi, outs):
        rows = slice(qi * Q_TILE, (qi + 1) * Q_TILE)
        o_t = jnp.concatenate([acc * (1.0 / l) for l, acc in outs], axis=0)
        o_ref[0, rows, :] = (o_t.T * g_ref[0, rows, :].astype(F32)).astype(BF16)

    _attention_pipeline(
        s // Q_TILE, B_HEADS_PER_STEP,
        lambda qi, st: q_ref[0, qi * Q_TILE:(qi + 1) * Q_TILE,
                             st * B_HEAD_PAD:(st + 1) * B_HEAD_PAD],
        lambda st: (lambda r0, r1: k_ref[0, r0:r1, st * B_HEAD_PAD:(st + 1) * B_HEAD_PAD]),
        lambda st: (lambda c0, c1: vt_ref[st * B_VDIM:(st + 1) * B_VDIM, c0:c1]),
        s_ref, p_ref, finalize)


def _b_attention(q, k, v, g):
    b, s, _ = q.shape
    qk_spec = pl.BlockSpec((1, s, B_HEADS_PER_STEP * B_HEAD_PAD), lambda bi, h: (bi, 0, h))
    spec = pl.BlockSpec((1, s, B_HEADS_PER_STEP * B_VDIM), lambda bi, h: (bi, 0, h))
    return pl.pallas_call(
        _b_attn_kernel,
        grid=(b, B_HEADS // B_HEADS_PER_STEP),
        in_specs=[qk_spec, qk_spec, spec, spec],
        out_specs=spec,
        out_shape=jax.ShapeDtypeStruct((b, s, B_WIDTH), BF16),
        scratch_shapes=[pltpu.VMEM((B_HEADS_PER_STEP * B_VDIM, s), BF16),
                        pltpu.VMEM((2 * B_HEADS_PER_STEP, s, Q_TILE), F32),
                        pltpu.VMEM((2 * B_HEADS_PER_STEP, s, Q_TILE), BF16)],
        compiler_params=pltpu.CompilerParams(vmem_limit_bytes=VMEM_LIMIT),
        name="b_attention",
    )(q, k, v, g)


def _pad_rows8(rows):
    out = jnp.zeros((8, LANES), F32)
    for r, vec in enumerate(rows):
        out = out.at[r, :vec.shape[0]].set(vec.astype(F32))
    return out


def _b_layouts(w_in, w_uq, w_ukv):
    d = w_in.shape[0]
    rope_cols = jnp.zeros((d, LANES), F32).at[:, B_NOPE:B_NOPE + B_ROPE].set(
        w_in[:, B_COL_ROPE:B_COL_ROPE + B_ROPE])
    w1 = jnp.concatenate([w_in[:, :B_COL_ROPE], rope_cols, w_in[:, B_COL_ROPE + B_ROPE:]], axis=1)
    uq = w_uq.reshape(B_Q_RANK, B_HEADS, B_NOPE + B_ROPE)
    uq = jnp.pad(uq, ((0, 0), (0, 0), (0, B_HEAD_PAD - B_NOPE - B_ROPE)))
    ukv = w_ukv.reshape(B_KV_RANK, B_HEADS, B_NOPE + B_VDIM)
    wk = jnp.pad(ukv[:, :, :B_NOPE], ((0, 0), (0, 0), (0, B_HEAD_PAD - B_NOPE)))
    wv = ukv[:, :, B_NOPE:]
    return (w1.astype(BF16), uq.reshape(B_Q_RANK, B_QK_PAD).astype(BF16),
            wk.reshape(B_KV_RANK, B_QK_PAD).astype(BF16),
            wv.reshape(B_KV_RANK, B_WIDTH).astype(BF16))


def kernel(x, c, positions, ada_w, ada_b, ln_g, ln_b, a_w_in, a_lambda_q1, a_lambda_k1,
           a_lambda_q2, a_lambda_k2, a_subln_g, a_w_out, b_w_in, b_q_norm_g, b_w_uq,
           b_kv_norm_g, b_w_ukv, b_w_out):
    b, s, d = x.shape
    assert d == D_MODEL and s % ROW_TILE == 0 and s % Q_TILE == 0 and Q_TILE == KV_TILE
    mod = _modulation(c, ada_w, ada_b).reshape(DEPTH, b, 3, d)
    pos3 = positions.reshape(b, s // LANES, LANES)

    lambda_init = 0.8 - 0.6 * math.exp(-0.3 * 0)
    q, k, v, g = _a_inproj(x, mod[0], pos3, _rope_consts(A_ROT, A_HEAD_DIM, 0),
                           a_w_in[0].astype(BF16))
    lam_rows = _pad_rows8([a_lambda_q1[0], a_lambda_k1[0], a_lambda_q2[0], a_lambda_k2[0]])
    o = _a_attention(q, k, v, g, lam_rows, _pad_rows8([a_subln_g[0]]), lambda_init)

    w1, wuq, wk, wv = _b_layouts(b_w_in[0], b_w_uq[0], b_w_ukv[0])
    x, q, k, v, g = _b_inproj(o, a_w_out[0].astype(BF16), x, mod[0], ln_g[0], ln_b[0], mod[1],
                              pos3, _rope_consts(B_ROPE, B_HEAD_PAD, B_NOPE), w1,
                              b_q_norm_g[0].reshape(1, B_Q_RANK), wuq,
                              b_kv_norm_g[0].reshape(1, B_KV_RANK), wk, wv)
    o = _b_attention(q, k, v, g)
    return _outproj_ln(o, b_w_out[0].astype(BF16), x, mod[1], ln_g[1], ln_b[1])
```

```python
import functools
import math

import jax
import jax.numpy as jnp
import numpy as np
from jax import lax
from jax.experimental import pallas as pl
from jax.experimental.pallas import tpu as pltpu

F32 = jnp.float32
BF16 = jnp.bfloat16

D_MODEL = 1024
CHUNK = 64
ROPE_THETA = 500000.0

A_HEADS = 8
A_HEAD_DIM = 64
A_ROT = A_HEAD_DIM // 4
A_WIDTH = A_HEADS * 2 * A_HEAD_DIM

B_HEADS = 16
B_NOPE = 64
B_ROPE = 32
B_VDIM = 64
B_Q_RANK = 512
B_KV_RANK = 256
B_WIDTH = B_HEADS * B_VDIM

DEPTH = 2
DEEPNORM_ALPHA = (2.0 * DEPTH) ** 0.25
LN_EPS = 1e-5
RMS_EPS = 1e-6
SUBLN_EPS = 1e-5
NEG_INF = -1e30

LANES = 128
FREQ_ROWS = 16
ROW_TILE = 512
OUT_ROW_TILE = 1024
Q_TILE = 256
KV_TILE = 256
SCORE_ROWS = 512
A_HEADS_PER_STEP = 2
B_HEADS_PER_STEP = 4
LOG2E = math.log2(math.e)
V7X_VMEM_BYTES = 64 * 1024 * 1024
VMEM_LIMIT = V7X_VMEM_BYTES // 4 * 3

_NT = (((1,), (1,)), ((), ()))


def _silu(x):
    return x * (1.0 / (1.0 + jnp.exp(-x)))


def _mod_kernel(c_ref, w_ref, b_ref, o_ref):
    ca = _silu(c_ref[...]).astype(BF16)
    o_ref[0] = jnp.dot(ca, w_ref[0].astype(BF16), preferred_element_type=F32) + b_ref[0]


def _modulation(c, ada_w, ada_b):
    depth, d, n3 = ada_w.shape
    b = c.shape[0]
    tn = 1024
    return pl.pallas_call(
        _mod_kernel,
        grid=(depth, n3 // tn),
        in_specs=[pl.BlockSpec((b, d), lambda i, j: (0, 0)),
                  pl.BlockSpec((1, d, tn), lambda i, j: (i, 0, j)),
                  pl.BlockSpec((1, 1, tn), lambda i, j: (i, 0, j))],
        out_specs=pl.BlockSpec((1, b, tn), lambda i, j: (i, 0, j)),
        out_shape=jax.ShapeDtypeStruct((depth, b, n3), F32),
        compiler_params=pltpu.CompilerParams(vmem_limit_bytes=VMEM_LIMIT),
        name="modulation",
    )(c, ada_w, ada_b.reshape(depth, 1, n3))


def _rope_tables(pos_ref, first_chunk, n_chunks, freq_ref, consts_ref):
    freq = freq_ref[...]
    cs, sn = [], []
    for r in range(n_chunks):
        pos = pos_ref[0, pl.ds(first_chunk + r, 1), :].astype(F32)
        ang = freq * pos
        reps = LANES // FREQ_ROWS
        cs.append(jnp.concatenate([jnp.cos(ang)] * reps, axis=0).T)
        sn.append(jnp.concatenate([jnp.sin(ang)] * reps, axis=0).T)
    cos_t = jnp.where(consts_ref[0:1, :] > 0.0, jnp.concatenate(cs, axis=0), 1.0)
    sin_t = jnp.concatenate(sn, axis=0)
    return cos_t, sin_t * consts_ref[1:2, :], sin_t * consts_ref[2:3, :]


def _rope_apply(a, cos_t, sin_up, sin_dn, half):
    return (a * cos_t + pltpu.roll(a, LANES - half, 1) * sin_up
            + pltpu.roll(a, half, 1) * sin_dn)


def _rope_consts(rot_dim, group, offset):
    half = rot_dim // 2
    assert FREQ_ROWS % half == 0 and offset % half == 0 and group % half == 0
    inv_freq = ROPE_THETA ** (-jnp.arange(0, rot_dim, 2, dtype=F32) / rot_dim)
    freq_rows = jnp.broadcast_to(jnp.tile(inv_freq, FREQ_ROWS // half)[:, None],
                                 (FREQ_ROWS, LANES))
    d = np.arange(LANES) % group - offset
    rotated = (d >= 0) & (d < rot_dim)
    first = rotated & (d < half)
    second = rotated & (d >= half)
    consts = np.zeros((8, LANES), np.float32)
    consts[0], consts[1], consts[2] = rotated, -first.astype(np.float32), second
    return freq_rows, jnp.asarray(consts)


def _a_inproj_kernel(x_ref, mod_ref, pos_ref, freq_ref, consts_ref, w_ref, q_ref, k_ref, v_ref,
                     g_ref):
    i = pl.program_id(1)
    shift = mod_ref[0, 0:1, :]
    scale = mod_ref[0, 1:2, :]
    u = (x_ref[0] * (1.0 + scale) + shift).astype(BF16)
    cos_t, sin_up, sin_dn = _rope_tables(pos_ref, i * (ROW_TILE // LANES), ROW_TILE // LANES,
                                         freq_ref, consts_ref)
    half = A_ROT // 2
    qk_scale = A_HEAD_DIM ** -0.5 * LOG2E

    def rope_store(col0, o_ref, mult):
        acc = jnp.dot(u, w_ref[:, col0:col0 + A_WIDTH], preferred_element_type=F32)
        for g in range(A_WIDTH // LANES):
            a = acc[:, g * LANES:(g + 1) * LANES]
            r = _rope_apply(a, cos_t, sin_up, sin_dn, half)
            if mult != 1.0:
                r = r * mult
            o_ref[0, :, g * LANES:(g + 1) * LANES] = r.astype(BF16)

    rope_store(0, q_ref, qk_scale)
    rope_store(A_WIDTH, k_ref, 1.0)
    v_ref[0] = jnp.dot(u, w_ref[:, 2 * A_WIDTH:3 * A_WIDTH],
                       preferred_element_type=F32).astype(BF16)
    gate = jnp.dot(u, w_ref[:, 3 * A_WIDTH:4 * A_WIDTH], preferred_element_type=F32)
    g_ref[0] = _silu(gate).astype(BF16)


def _a_inproj(x, mod, pos3, rope, w_bf16):
    b, s, d = x.shape
    freq, consts = rope
    n = w_bf16.shape[1]
    out = jax.ShapeDtypeStruct((b, s, A_WIDTH), BF16)
    ospec = pl.BlockSpec((1, ROW_TILE, A_WIDTH), lambda bi, i: (bi, i, 0))
    return pl.pallas_call(
        _a_inproj_kernel,
        grid=(b, s // ROW_TILE),
        in_specs=[pl.BlockSpec((1, ROW_TILE, d), lambda bi, i: (bi, i, 0)),
                  pl.BlockSpec((1, 3, d), lambda bi, i: (bi, 0, 0)),
                  pl.BlockSpec((1, s // LANES, LANES), lambda bi, i: (bi, 0, 0)),
                  pl.BlockSpec((FREQ_ROWS, LANES), lambda bi, i: (0, 0)),
                  pl.BlockSpec((8, LANES), lambda bi, i: (0, 0)),
                  pl.BlockSpec((d, n), lambda bi, i: (0, 0))],
        out_specs=[ospec, ospec, ospec, ospec],
        out_shape=[out, out, out, out],
        compiler_params=pltpu.CompilerParams(vmem_limit_bytes=VMEM_LIMIT),
        name="a_inproj",
    )(x, mod, pos3, freq, consts, w_bf16)


def _chunk_mask():
    krow = lax.broadcasted_iota(jnp.int32, (KV_TILE, Q_TILE), 0) // CHUNK
    qcol = lax.broadcasted_iota(jnp.int32, (KV_TILE, Q_TILE), 1) // CHUNK
    return krow <= qcol


def _score_items(q, k_rows, kv_len, s_ref, mask, out):
    diag = kv_len - KV_TILE

    def boundary():
        s_d = lax.dot_general(k_rows(diag, kv_len), q, _NT, preferred_element_type=F32)
        s_d = jnp.where(mask, s_d, NEG_INF)
        s_ref[diag:kv_len, :] = s_d
        out["m"] = jnp.max(s_d, axis=0, keepdims=True)

    def interior(r0, r1):
        s_o = lax.dot_general(k_rows(r0, r1), q, _NT, preferred_element_type=F32)
        s_ref[r0:r1, :] = s_o
        out["m"] = jnp.maximum(out["m"], jnp.max(s_o, axis=0, keepdims=True))

    items = [boundary]
    for r0 in range(0, diag, SCORE_ROWS):
        items.append(functools.partial(interior, r0, min(r0 + SCORE_ROWS, diag)))
    return items


def _exp_items(kv_len, s_ref, p_ref, state):
    def block(r0):
        p = jnp.exp2(s_ref[r0:r0 + KV_TILE, :] - state["m"])
        part = jnp.sum(p, axis=0, keepdims=True)
        state["l"] = part if "l" not in state else state["l"] + part
        p_ref[r0:r0 + KV_TILE, :] = p.astype(BF16)

    return [functools.partial(block, r0) for r0 in range(0, kv_len, KV_TILE)]


def _emit_interleaved(item_lists):
    item_lists = [items for items in item_lists if items]
    done = [0] * len(item_lists)
    for _ in range(sum(len(items) for items in item_lists)):
        i = min((i for i in range(len(item_lists)) if done[i] < len(item_lists[i])),
                key=lambda i: (done[i] + 0.5) / len(item_lists[i]))
        item_lists[i][done[i]]()
        done[i] += 1


def _attention_pipeline(n_q, n_streams, q_of, k_rows_of, vt_cols_of, s_ref, p_ref, finalize):
    mask = _chunk_mask()
    state = {}

    def slot(qi, st):
        return (qi % 2) * n_streams + st

    def pv_item(qi, st):
        kv_len = (qi + 1) * Q_TILE
        state[qi, st]["acc"] = jnp.dot(vt_cols_of(st)(0, kv_len),
                                       p_ref[slot(qi, st), 0:kv_len, :],
                                       preferred_element_type=F32)

    order = list(range(n_q))
    for t in range(n_q + 2):
        scores, exps, pvs = [], [], []
        for st in range(n_streams):
            if t < n_q:
                qi = order[t]
                state[qi, st] = {}
                scores.append(_score_items(q_of(qi, st), k_rows_of(st), (qi + 1) * Q_TILE,
                                           s_ref.at[slot(qi, st)], mask, state[qi, st]))
            if 0 <= t - 1 < n_q:
                qi = order[t - 1]
                exps.append(_exp_items((qi + 1) * Q_TILE, s_ref.at[slot(qi, st)],
                                       p_ref.at[slot(qi, st)], state[qi, st]))
            if 0 <= t - 2 < n_q:
                pvs.append([functools.partial(pv_item, order[t - 2], st)])
        _emit_interleaved(scores + exps + pvs)
        if 0 <= t - 2 < n_q:
            qi = order[t - 2]
            done = [state.pop((qi, st)) for st in range(n_streams)]
            finalize(qi, [(d["l"], d["acc"]) for d in done])


def _store_transposed_values(v_ref, vt_ref):
    for r0 in range(0, v_ref.shape[1], KV_TILE):
        blk = v_ref[0, r0:r0 + KV_TILE, :].astype(F32)
        vt_ref[:, r0:r0 + KV_TILE] = blk.T.astype(BF16)


def _a_attn_kernel(lambda_init, q_ref, k_ref, v_ref, g_ref, lam_ref, subg_ref, o_ref,
                   kp_ref, vt_ref, s_ref, p_ref):
    s = q_ref.shape[1]
    lane = lax.broadcasted_iota(jnp.int32, (KV_TILE, LANES), 1)
    first_map = lane < A_HEAD_DIM
    for r0 in range(0, s, KV_TILE):
        for hh in range(A_HEADS_PER_STEP):
            kb = k_ref[0, r0:r0 + KV_TILE, hh * LANES:(hh + 1) * LANES].astype(F32)
            kp_ref[2 * hh, r0:r0 + KV_TILE, :] = jnp.where(first_map, kb, 0.0).astype(BF16)
            kp_ref[2 * hh + 1, r0:r0 + KV_TILE, :] = jnp.where(first_map, 0.0, kb).astype(BF16)
    dv = 2 * A_HEAD_DIM
    _store_transposed_values(v_ref, vt_ref)

    lam = (jnp.exp(jnp.sum(lam_ref[0:1, :] * lam_ref[1:2, :], axis=1, keepdims=True))
           - jnp.exp(jnp.sum(lam_ref[2:3, :] * lam_ref[3:4, :], axis=1, keepdims=True))
           + lambda_init)
    out_gain = subg_ref[0:1, :] * (1.0 - lambda_init)

    def finalize(qi, outs):
        rows = slice(qi * Q_TILE, (qi + 1) * Q_TILE)
        for hh in range(A_HEADS_PER_STEP):
            cols = slice(hh * LANES, (hh + 1) * LANES)
            (l1, acc1), (l2, acc2) = outs[2 * hh:2 * hh + 2]
            o_t = acc1 * (1.0 / l1) - acc2 * (lam / l2)
            ms = jnp.mean(o_t * o_t, axis=0, keepdims=True)
            o_t = o_t * lax.rsqrt(ms + SUBLN_EPS)
            o = o_t.T * out_gain * g_ref[0, rows, cols].astype(F32)
            o_ref[0, rows, cols] = o.astype(BF16)

    _attention_pipeline(
        s // Q_TILE, 2 * A_HEADS_PER_STEP,
        lambda qi, st: q_ref[0, qi * Q_TILE:(qi + 1) * Q_TILE,
                             (st // 2) * LANES:(st // 2 + 1) * LANES],
        lambda st: (lambda r0, r1: kp_ref[st, r0:r1, :]),
        lambda st: (lambda c0, c1: vt_ref[(st // 2) * dv:(st // 2 + 1) * dv, c0:c1]),
        s_ref, p_ref, finalize)


def _a_attention(q, k, v, g, lam_rows, subg_rows, lambda_init):
    b, s, _ = q.shape
    n_streams = 2 * A_HEADS_PER_STEP
    spec = pl.BlockSpec((1, s, A_HEADS_PER_STEP * LANES), lambda bi, h: (bi, 0, h))
    small = pl.BlockSpec((8, LANES), lambda bi, h: (0, 0))
    return pl.pallas_call(
        functools.partial(_a_attn_kernel, lambda_init),
        grid=(b, A_HEADS // A_HEADS_PER_STEP),
        in_specs=[spec, spec, spec, spec, small, small],
        out_specs=spec,
        out_shape=jax.ShapeDtypeStruct((b, s, A_WIDTH), BF16),
        scratch_shapes=[pltpu.VMEM((n_streams, s, LANES), BF16),
                        pltpu.VMEM((A_HEADS_PER_STEP * LANES, s), BF16),
                        pltpu.VMEM((2 * n_streams, s, Q_TILE), F32),
                        pltpu.VMEM((2 * n_streams, s, Q_TILE), BF16)],
        compiler_params=pltpu.CompilerParams(vmem_limit_bytes=VMEM_LIMIT),
        name="a_attention",
    )(q, k, v, g, lam_rows, subg_rows)


def _deepnorm(y, x, mod_ref, lng_ref, lnb_ref):
    gate = mod_ref[0, 2:3, :]
    z = DEEPNORM_ALPHA * x + gate * y
    mu = jnp.mean(z, axis=1, keepdims=True)
    zc = z - mu
    var = jnp.mean(zc * zc, axis=1, keepdims=True)
    return zc * lax.rsqrt(var + LN_EPS) * lng_ref[...] + lnb_ref[...]


def _outproj_ln_kernel(o_ref, w_ref, x_ref, mod_ref, lng_ref, lnb_ref, y_ref):
    y = jnp.dot(o_ref[0], w_ref[...], preferred_element_type=F32)
    y_ref[0] = _deepnorm(y, x_ref[0], mod_ref, lng_ref, lnb_ref)


def _outproj_ln(o, w_bf16, x, mod, ln_g, ln_b):
    b, s, d = x.shape
    width = o.shape[2]
    row = pl.BlockSpec((1, d), lambda bi, i: (0, 0))
    return pl.pallas_call(
        _outproj_ln_kernel,
        grid=(b, s // OUT_ROW_TILE),
        in_specs=[pl.BlockSpec((1, OUT_ROW_TILE, width), lambda bi, i: (bi, i, 0)),
                  pl.BlockSpec((width, d), lambda bi, i: (0, 0)),
                  pl.BlockSpec((1, OUT_ROW_TILE, d), lambda bi, i: (bi, i, 0)),
                  pl.BlockSpec((1, 3, d), lambda bi, i: (bi, 0, 0)),
                  row, row],
        out_specs=pl.BlockSpec((1, OUT_ROW_TILE, d), lambda bi, i: (bi, i, 0)),
        out_shape=jax.ShapeDtypeStruct((b, s, d), F32),
        compiler_params=pltpu.CompilerParams(vmem_limit_bytes=VMEM_LIMIT),
        name="outproj_ln",
    )(o, w_bf16, x, mod, ln_g.reshape(1, d), ln_b.reshape(1, d))


B_HEAD_PAD = LANES
B_QK_PAD = B_HEADS * B_HEAD_PAD
B_COL_KV = B_Q_RANK
B_COL_ROPE = B_Q_RANK + B_KV_RANK
B_COL_GATE = B_COL_ROPE + LANES
B_IN_PAD = B_COL_GATE + B_WIDTH
B_COL_CHUNK = 512


def _rms(x, g_row, eps):
    ms = jnp.mean(x * x, axis=1, keepdims=True)
    return x * lax.rsqrt(ms + eps) * g_row


def _b_inproj_kernel(o_ref, wo_ref, x0_ref, mod0_ref, lng_ref, lnb_ref, mod_ref, pos_ref, freq_ref,
                     consts_ref, w1_ref, qg_ref, wuq_ref, kvg_ref, wk_ref, wv_ref,
                     x_ref, q_ref, k_ref, v_ref, g_ref):
    i = pl.program_id(1)
    shift = mod_ref[0, 0:1, :]
    scale = mod_ref[0, 1:2, :]
    y = jnp.dot(o_ref[0], wo_ref[...], preferred_element_type=F32)
    x = _deepnorm(y, x0_ref[0], mod0_ref, lng_ref, lnb_ref)
    x_ref[0] = x
    u = (x * (1.0 + scale) + shift).astype(BF16)
    cos_t, sin_up, sin_dn = _rope_tables(pos_ref, i * (ROW_TILE // LANES), ROW_TILE // LANES,
                                         freq_ref, consts_ref)
    half = B_ROPE // 2
    qk_scale = (B_NOPE + B_ROPE) ** -0.5 * LOG2E

    lat = jnp.dot(u, w1_ref[:, 0:B_COL_GATE], preferred_element_type=F32)
    qn = _rms(lat[:, 0:B_Q_RANK], qg_ref[...], RMS_EPS).astype(BF16)
    kvn = _rms(lat[:, B_COL_KV:B_COL_ROPE], kvg_ref[...], RMS_EPS).astype(BF16)
    k_rope = _rope_apply(lat[:, B_COL_ROPE:B_COL_GATE], cos_t, sin_up, sin_dn, half)

    cos_q, up_q, dn_q = cos_t * qk_scale, sin_up * qk_scale, sin_dn * qk_scale
    for c0 in range(0, B_QK_PAD, B_COL_CHUNK):
        qc = jnp.dot(qn, wuq_ref[:, c0:c0 + B_COL_CHUNK], preferred_element_type=F32)
        kc = jnp.dot(kvn, wk_ref[:, c0:c0 + B_COL_CHUNK], preferred_element_type=F32)
        for g0 in range(0, B_COL_CHUNK, B_HEAD_PAD):
            cols = slice(c0 + g0, c0 + g0 + B_HEAD_PAD)
            q_ref[0, :, cols] = _rope_apply(qc[:, g0:g0 + B_HEAD_PAD], cos_q, up_q, dn_q,
                                            half).astype(BF16)
            k_ref[0, :, cols] = (kc[:, g0:g0 + B_HEAD_PAD] + k_rope).astype(BF16)
    v_ref[0] = jnp.dot(kvn, wv_ref[...], preferred_element_type=F32).astype(BF16)
    gate = jnp.dot(u, w1_ref[:, B_COL_GATE:B_IN_PAD], preferred_element_type=F32)
    g_ref[0] = _silu(gate).astype(BF16)


def _b_inproj(o, wo, x0, mod0, ln_g, ln_b, mod, pos3, rope, w1, qg, wuq, kvg, wk, wv):
    b, s, d = x0.shape
    freq, consts = rope

    def full(a):
        return pl.BlockSpec(a.shape, lambda bi, i: (0,) * a.ndim,
                            pipeline_mode=pl.Buffered(1))

    def rows(width):
        return pl.BlockSpec((1, ROW_TILE, width), lambda bi, i: (bi, i, 0))

    mod_spec = pl.BlockSpec((1, 3, d), lambda bi, i: (bi, 0, 0))
    ln_g, ln_b = ln_g.reshape(1, d), ln_b.reshape(1, d)
    return pl.pallas_call(
        _b_inproj_kernel,
        grid=(b, s // ROW_TILE),
        in_specs=[rows(o.shape[2]), full(wo), rows(d), mod_spec, full(ln_g), full(ln_b),
                  mod_spec,
                  pl.BlockSpec((1, s // LANES, LANES), lambda bi, i: (bi, 0, 0)),
                  full(freq), full(consts), full(w1), full(qg), full(wuq), full(kvg), full(wk),
                  full(wv)],
        out_specs=[rows(d), rows(B_QK_PAD), rows(B_QK_PAD), rows(B_WIDTH), rows(B_WIDTH)],
        out_shape=[jax.ShapeDtypeStruct((b, s, d), F32),
                   jax.ShapeDtypeStruct((b, s, B_QK_PAD), BF16),
                   jax.ShapeDtypeStruct((b, s, B_QK_PAD), BF16),
                   jax.ShapeDtypeStruct((b, s, B_WIDTH), BF16),
                   jax.ShapeDtypeStruct((b, s, B_WIDTH), BF16)],
        compiler_params=pltpu.CompilerParams(vmem_limit_bytes=VMEM_LIMIT),
        name="b_inproj",
    )(o, wo, x0, mod0, ln_g, ln_b, mod, pos3, freq, consts, w1, qg, wuq, kvg, wk, wv)


def _b_attn_kernel(q_ref, k_ref, v_ref, g_ref, o_ref, vt_ref, s_ref, p_ref):
    s = q_ref.shape[1]
    _store_transposed_values(v_ref, vt_ref)

    def finalize(qi, outs):
        rows = slice(qi * Q_TILE, (qi + 1) * Q_TILE)
        o_t = jnp.concatenate([acc * (1.0 / l) for l, acc in outs], axis=0)
        o_ref[0, rows, :] = (o_t.T * g_ref[0, rows, :].astype(F32)).astype(BF16)

    _attention_pipeline(
        s // Q_TILE, B_HEADS_PER_STEP,
        lambda qi, st: q_ref[0, qi * Q_TILE:(qi + 1) * Q_TILE,
                             st * B_HEAD_PAD:(st + 1) * B_HEAD_PAD],
        lambda st: (lambda r0, r1: k_ref[0, r0:r1, st * B_HEAD_PAD:(st + 1) * B_HEAD_PAD]),
        lambda st: (lambda c0, c1: vt_ref[st * B_VDIM:(st + 1) * B_VDIM, c0:c1]),
        s_ref, p_ref, finalize)


def _b_attention(q, k, v, g):
    b, s, _ = q.shape
    qk_spec = pl.BlockSpec((1, s, B_HEADS_PER_STEP * B_HEAD_PAD), lambda bi, h: (bi, 0, h))
    spec = pl.BlockSpec((1, s, B_HEADS_PER_STEP * B_VDIM), lambda bi, h: (bi, 0, h))
    return pl.pallas_call(
        _b_attn_kernel,
        grid=(b, B_HEADS // B_HEADS_PER_STEP),
        in_specs=[qk_spec, qk_spec, spec, spec],
        out_specs=spec,
        out_shape=jax.ShapeDtypeStruct((b, s, B_WIDTH), BF16),
        scratch_shapes=[pltpu.VMEM((B_HEADS_PER_STEP * B_VDIM, s), BF16),
                        pltpu.VMEM((2 * B_HEADS_PER_STEP, s, Q_TILE), F32),
                        pltpu.VMEM((2 * B_HEADS_PER_STEP, s, Q_TILE), BF16)],
        compiler_params=pltpu.CompilerParams(vmem_limit_bytes=VMEM_LIMIT),
        name="b_attention",
    )(q, k, v, g)


def _pad_rows8(rows):
    out = jnp.zeros((8, LANES), F32)
    for r, vec in enumerate(rows):
        out = out.at[r, :vec.shape[0]].set(vec.astype(F32))
    return out


def _b_layouts(w_in, w_uq, w_ukv):
    d = w_in.shape[0]
    rope_cols = jnp.zeros((d, LANES), F32).at[:, B_NOPE:B_NOPE + B_ROPE].set(
        w_in[:, B_COL_ROPE:B_COL_ROPE + B_ROPE])
    w1 = jnp.concatenate([w_in[:, :B_COL_ROPE], rope_cols, w_in[:, B_COL_ROPE + B_ROPE:]], axis=1)
    uq = w_uq.reshape(B_Q_RANK, B_HEADS, B_NOPE + B_ROPE)
    uq = jnp.pad(uq, ((0, 0), (0, 0), (0, B_HEAD_PAD - B_NOPE - B_ROPE)))
    ukv = w_ukv.reshape(B_KV_RANK, B_HEADS, B_NOPE + B_VDIM)
    wk = jnp.pad(ukv[:, :, :B_NOPE], ((0, 0), (0, 0), (0, B_HEAD_PAD - B_NOPE)))
    wv = ukv[:, :, B_NOPE:]
    return (w1.astype(BF16), uq.reshape(B_Q_RANK, B_QK_PAD).astype(BF16),
            wk.reshape(B_KV_RANK, B_QK_PAD).astype(BF16),
            wv.reshape(B_KV_RANK, B_WIDTH).astype(BF16))


def kernel(x, c, positions, ada_w, ada_b, ln_g, ln_b, a_w_in, a_lambda_q1, a_lambda_k1,
           a_lambda_q2, a_lambda_k2, a_subln_g, a_w_out, b_w_in, b_q_norm_g, b_w_uq,
           b_kv_norm_g, b_w_ukv, b_w_out):
    b, s, d = x.shape
    assert d == D_MODEL and s % ROW_TILE == 0 and s % Q_TILE == 0 and Q_TILE == KV_TILE
    mod = _modulation(c, ada_w, ada_b).reshape(DEPTH, b, 3, d)
    pos3 = positions.reshape(b, s // LANES, LANES)

    lambda_init = 0.8 - 0.6 * math.exp(-0.3 * 0)
    q, k, v, g = _a_inproj(x, mod[0], pos3, _rope_consts(A_ROT, A_HEAD_DIM, 0),
                           a_w_in[0].astype(BF16))
    lam_rows = _pad_rows8([a_lambda_q1[0], a_lambda_k1[0], a_lambda_q2[0], a_lambda_k2[0]])
    o = _a_attention(q, k, v, g, lam_rows, _pad_rows8([a_subln_g[0]]), lambda_init)

    w1, wuq, wk, wv = _b_layouts(b_w_in[0], b_w_uq[0], b_w_ukv[0])
    x, q, k, v, g = _b_inproj(o, a_w_out[0].astype(BF16), x, mod[0], ln_g[0], ln_b[0], mod[1],
                              pos3, _rope_consts(B_ROPE, B_HEAD_PAD, B_NOPE), w1,
                              b_q_norm_g[0].reshape(1, B_Q_RANK), wuq,
                              b_kv_norm_g[0].reshape(1, B_KV_RANK), wk, wv)
    o = _b_attention(q, k, v, g)
    return _outproj_ln(o, b_w_out[0].astype(BF16), x, mod[1], ln_g[1], ln_b[1])
```

```python
import functools
import math

import jax
import jax.numpy as jnp
import numpy as np
from jax import lax
from jax.experimental import pallas as pl
from jax.experimental.pallas import tpu as pltpu

F32 = jnp.float32
BF16 = jnp.bfloat16

D_MODEL = 1024
CHUNK = 64
ROPE_THETA = 500000.0

A_HEADS = 8
A_HEAD_DIM = 64
A_ROT = A_HEAD_DIM // 4
A_WIDTH = A_HEADS * 2 * A_HEAD_DIM

B_HEADS = 16
B_NOPE = 64
B_ROPE = 32
B_VDIM = 64
B_Q_RANK = 512
B_KV_RANK = 256
B_WIDTH = B_HEADS * B_VDIM

DEPTH = 2
DEEPNORM_ALPHA = (2.0 * DEPTH) ** 0.25
LN_EPS = 1e-5
RMS_EPS = 1e-6
SUBLN_EPS = 1e-5
NEG_INF = -1e30

LANES = 128
FREQ_ROWS = 16
ROW_TILE = 512
OUT_ROW_TILE = 1024
Q_TILE = 256
KV_TILE = 256
SCORE_ROWS = 512
A_HEADS_PER_STEP = 2
B_HEADS_PER_STEP = 4
LOG2E = math.log2(math.e)
V7X_VMEM_BYTES = 64 * 1024 * 1024
VMEM_LIMIT = V7X_VMEM_BYTES // 4 * 3

_NT = (((1,), (1,)), ((), ()))


def _silu(x):
    return x * (1.0 / (1.0 + jnp.exp(-x)))


def _mod_kernel(c_ref, w_ref, b_ref, o_ref):
    ca = _silu(c_ref[...]).astype(BF16)
    o_ref[0] = jnp.dot(ca, w_ref[0].astype(BF16), preferred_element_type=F32) + b_ref[0]


def _modulation(c, ada_w, ada_b):
    depth, d, n3 = ada_w.shape
    b = c.shape[0]
    tn = 1024
    return pl.pallas_call(
        _mod_kernel,
        grid=(depth, n3 // tn),
        in_specs=[pl.BlockSpec((b, d), lambda i, j: (0, 0)),
                  pl.BlockSpec((1, d, tn), lambda i, j: (i, 0, j)),
                  pl.BlockSpec((1, 1, tn), lambda i, j: (i, 0, j))],
        out_specs=pl.BlockSpec((1, b, tn), lambda i, j: (i, 0, j)),
        out_shape=jax.ShapeDtypeStruct((depth, b, n3), F32),
        compiler_params=pltpu.CompilerParams(vmem_limit_bytes=VMEM_LIMIT),
        name="modulation",
    )(c, ada_w, ada_b.reshape(depth, 1, n3))


def _rope_tables(pos_ref, first_chunk, n_chunks, freq_ref, consts_ref):
    freq = freq_ref[...]
    cs, sn = [], []
    for r in range(n_chunks):
        pos = pos_ref[0, pl.ds(first_chunk + r, 1), :].astype(F32)
        ang = freq * pos
        reps = LANES // FREQ_ROWS
        cs.append(jnp.concatenate([jnp.cos(ang)] * reps, axis=0).T)
        sn.append(jnp.concatenate([jnp.sin(ang)] * reps, axis=0).T)
    cos_t = jnp.where(consts_ref[0:1, :] > 0.0, jnp.concatenate(cs, axis=0), 1.0)
    sin_t = jnp.concatenate(sn, axis=0)
    return cos_t, sin_t * consts_ref[1:2, :], sin_t * consts_ref[2:3, :]


def _rope_apply(a, cos_t, sin_up, sin_dn, half):
    return (a * cos_t + pltpu.roll(a, LANES - half, 1) * sin_up
            + pltpu.roll(a, half, 1) * sin_dn)


def _rope_consts(rot_dim, group, offset):
    half = rot_dim // 2
    assert FREQ_ROWS % half == 0 and offset % half == 0 and group % half == 0
    inv_freq = ROPE_THETA ** (-jnp.arange(0, rot_dim, 2, dtype=F32) / rot_dim)
    freq_rows = jnp.broadcast_to(jnp.tile(inv_freq, FREQ_ROWS // half)[:, None],
                                 (FREQ_ROWS, LANES))
    d = np.arange(LANES) % group - offset
    rotated = (d >= 0) & (d < rot_dim)
    first = rotated & (d < half)
    second = rotated & (d >= half)
    consts = np.zeros((8, LANES), np.float32)
    consts[0], consts[1], consts[2] = rotated, -first.astype(np.float32), second
    return freq_rows, jnp.asarray(consts)


def _a_inproj_kernel(x_ref, mod_ref, pos_ref, freq_ref, consts_ref, w_ref, q_ref, k_ref, v_ref,
                     g_ref):
    i = pl.program_id(1)
    shift = mod_ref[0, 0:1, :]
    scale = mod_ref[0, 1:2, :]
    u = (x_ref[0] * (1.0 + scale) + shift).astype(BF16)
    cos_t, sin_up, sin_dn = _rope_tables(pos_ref, i * (ROW_TILE // LANES), ROW_TILE // LANES,
                                         freq_ref, consts_ref)
    half = A_ROT // 2
    qk_scale = A_HEAD_DIM ** -0.5 * LOG2E

    def rope_store(col0, o_ref, mult):
        acc = jnp.dot(u, w_ref[:, col0:col0 + A_WIDTH], preferred_element_type=F32)
        for g in range(A_WIDTH // LANES):
            a = acc[:, g * LANES:(g + 1) * LANES]
            r = _rope_apply(a, cos_t, sin_up, sin_dn, half)
            if mult != 1.0:
                r = r * mult
            o_ref[0, :, g * LANES:(g + 1) * LANES] = r.astype(BF16)

    rope_store(0, q_ref, qk_scale)
    rope_store(A_WIDTH, k_ref, 1.0)
    v_ref[0] = jnp.dot(u, w_ref[:, 2 * A_WIDTH:3 * A_WIDTH],
                       preferred_element_type=F32).astype(BF16)
    gate = jnp.dot(u, w_ref[:, 3 * A_WIDTH:4 * A_WIDTH], preferred_element_type=F32)
    g_ref[0] = _silu(gate).astype(BF16)


def _a_inproj(x, mod, pos3, rope, w_bf16):
    b, s, d = x.shape
    freq, consts = rope
    n = w_bf16.shape[1]
    out = jax.ShapeDtypeStruct((b, s, A_WIDTH), BF16)
    ospec = pl.BlockSpec((1, ROW_TILE, A_WIDTH), lambda bi, i: (bi, i, 0))
    return pl.pallas_call(
        _a_inproj_kernel,
        grid=(b, s // ROW_TILE),
        in_specs=[pl.BlockSpec((1, ROW_TILE, d), lambda bi, i: (bi, i, 0)),
                  pl.BlockSpec((1, 3, d), lambda bi, i: (bi, 0, 0)),
                  pl.BlockSpec((1, s // LANES, LANES), lambda bi, i: (bi, 0, 0)),
                  pl.BlockSpec((FREQ_ROWS, LANES), lambda bi, i: (0, 0)),
                  pl.BlockSpec((8, LANES), lambda bi, i: (0, 0)),
                  pl.BlockSpec((d, n), lambda bi, i: (0, 0))],
        out_specs=[ospec, ospec, ospec, ospec],
        out_shape=[out, out, out, out],
        compiler_params=pltpu.CompilerParams(vmem_limit_bytes=VMEM_LIMIT),
        name="a_inproj",
    )(x, mod, pos3, freq, consts, w_bf16)


def _chunk_mask():
    krow = lax.broadcasted_iota(jnp.int32, (KV_TILE, Q_TILE), 0) // CHUNK
    qcol = lax.broadcasted_iota(jnp.int32, (KV_TILE, Q_TILE), 1) // CHUNK
    return krow <= qcol


def _score_items(q, k_rows, kv_len, s_ref, mask, out):
    diag = kv_len - KV_TILE

    def boundary():
        s_d = lax.dot_general(k_rows(diag, kv_len), q, _NT, preferred_element_type=F32)
        s_d = jnp.where(mask, s_d, NEG_INF)
        s_ref[diag:kv_len, :] = s_d
        out["m"] = jnp.max(s_d, axis=0, keepdims=True)

    def interior(r0, r1):
        s_o = lax.dot_general(k_rows(r0, r1), q, _NT, preferred_element_type=F32)
        s_ref[r0:r1, :] = s_o
        out["m"] = jnp.maximum(out["m"], jnp.max(s_o, axis=0, keepdims=True))

    items = [boundary]
    for r0 in range(0, diag, SCORE_ROWS):
        items.append(functools.partial(interior, r0, min(r0 + SCORE_ROWS, diag)))
    return items


def _exp_items(kv_len, s_ref, p_ref, state):
    def block(r0):
        p = jnp.exp2(s_ref[r0:r0 + KV_TILE, :] - state["m"])
        part = jnp.sum(p, axis=0, keepdims=True)
        state["l"] = part if "l" not in state else state["l"] + part
        p_ref[r0:r0 + KV_TILE, :] = p.astype(BF16)

    return [functools.partial(block, r0) for r0 in range(0, kv_len, KV_TILE)]


def _emit_interleaved(item_lists):
    item_lists = [items for items in item_lists if items]
    done = [0] * len(item_lists)
    for _ in range(sum(len(items) for items in item_lists)):
        i = min((i for i in range(len(item_lists)) if done[i] < len(item_lists[i])),
                key=lambda i: (done[i] + 0.5) / len(item_lists[i]))
        item_lists[i][done[i]]()
        done[i] += 1


def _attention_pipeline(n_q, n_streams, q_of, k_rows_of, vt_cols_of, s_ref, p_ref, finalize):
    mask = _chunk_mask()
    state = {}

    def slot(qi, st):
        return (qi % 2) * n_streams + st

    def pv_item(qi, st):
        kv_len = (qi + 1) * Q_TILE
        state[qi, st]["acc"] = jnp.dot(vt_cols_of(st)(0, kv_len),
                                       p_ref[slot(qi, st), 0:kv_len, :],
                                       preferred_element_type=F32)

    order = list(range(n_q))
    for t in range(n_q + 2):
        scores, exps, pvs = [], [], []
        for st in range(n_streams):
            if t < n_q:
                qi = order[t]
                state[qi, st] = {}
                scores.append(_score_items(q_of(qi, st), k_rows_of(st), (qi + 1) * Q_TILE,
                                           s_ref.at[slot(qi, st)], mask, state[qi, st]))
            if 0 <= t - 1 < n_q:
                qi = order[t - 1]
                exps.append(_exp_items((qi + 1) * Q_TILE, s_ref.at[slot(qi, st)],
                                       p_ref.at[slot(qi, st)], state[qi, st]))
            if 0 <= t - 2 < n_q:
                pvs.append([functools.partial(pv_item, order[t - 2], st)])
        _emit_interleaved(scores + exps + pvs)
        if 0 <= t - 2 < n_q:
            qi = order[t - 2]
            done = [state.pop((qi, st)) for st in range(n_streams)]
            finalize(qi, [(d["l"], d["acc"]) for d in done])


def _store_transposed_values(v_ref, vt_ref):
    for r0 in range(0, v_ref.shape[1], KV_TILE):
        blk = v_ref[0, r0:r0 + KV_TILE, :].astype(F32)
        vt_ref[:, r0:r0 + KV_TILE] = blk.T.astype(BF16)


def _a_attn_kernel(lambda_init, q_ref, k_ref, v_ref, g_ref, lam_ref, subg_ref, o_ref,
                   vt_ref, s_ref, p_ref):
    s = q_ref.shape[1]
    first_map = lax.broadcasted_iota(jnp.int32, (Q_TILE, LANES), 1) < A_HEAD_DIM
    dv = 2 * A_HEAD_DIM
    _store_transposed_values(v_ref, vt_ref)

    def map_query(qi, st):
        hh, mp = divmod(st, 2)
        q = q_ref[0, qi * Q_TILE:(qi + 1) * Q_TILE, hh * LANES:(hh + 1) * LANES].astype(F32)
        return jnp.where(first_map == (mp == 0), q, 0.0).astype(BF16)

    lam = (jnp.exp(jnp.sum(lam_ref[0:1, :] * lam_ref[1:2, :], axis=1, keepdims=True))
           - jnp.exp(jnp.sum(lam_ref[2:3, :] * lam_ref[3:4, :], axis=1, keepdims=True))
           + lambda_init)
    out_gain = subg_ref[0:1, :] * (1.0 - lambda_init)

    def finalize(qi, outs):
        rows = slice(qi * Q_TILE, (qi + 1) * Q_TILE)
        for hh in range(A_HEADS_PER_STEP):
            cols = slice(hh * LANES, (hh + 1) * LANES)
            (l1, acc1), (l2, acc2) = outs[2 * hh:2 * hh + 2]
            o_t = acc1 * (1.0 / l1) - acc2 * (lam / l2)
            ms = jnp.mean(o_t * o_t, axis=0, keepdims=True)
            o_t = o_t * lax.rsqrt(ms + SUBLN_EPS)
            o = o_t.T * out_gain * g_ref[0, rows, cols].astype(F32)
            o_ref[0, rows, cols] = o.astype(BF16)

    _attention_pipeline(
        s // Q_TILE, 2 * A_HEADS_PER_STEP, map_query,
        lambda st: (lambda r0, r1: k_ref[0, r0:r1, (st // 2) * LANES:(st // 2 + 1) * LANES]),
        lambda st: (lambda c0, c1: vt_ref[(st // 2) * dv:(st // 2 + 1) * dv, c0:c1]),
        s_ref, p_ref, finalize)


def _a_attention(q, k, v, g, lam_rows, subg_rows, lambda_init):
    b, s, _ = q.shape
    n_streams = 2 * A_HEADS_PER_STEP
    spec = pl.BlockSpec((1, s, A_HEADS_PER_STEP * LANES), lambda bi, h: (bi, 0, h))
    small = pl.BlockSpec((8, LANES), lambda bi, h: (0, 0))
    return pl.pallas_call(
        functools.partial(_a_attn_kernel, lambda_init),
        grid=(b, A_HEADS // A_HEADS_PER_STEP),
        in_specs=[spec, spec, spec, spec, small, small],
        out_specs=spec,
        out_shape=jax.ShapeDtypeStruct((b, s, A_WIDTH), BF16),
        scratch_shapes=[pltpu.VMEM((A_HEADS_PER_STEP * LANES, s), BF16),
                        pltpu.VMEM((2 * n_streams, s, Q_TILE), F32),
                        pltpu.VMEM((2 * n_streams, s, Q_TILE), BF16)],
        compiler_params=pltpu.CompilerParams(vmem_limit_bytes=VMEM_LIMIT),
        name="a_attention",
    )(q, k, v, g, lam_rows, subg_rows)


def _deepnorm(y, x, mod_ref, lng_ref, lnb_ref):
    gate = mod_ref[0, 2:3, :]
    z = DEEPNORM_ALPHA * x + gate * y
    mu = jnp.mean(z, axis=1, keepdims=True)
    zc = z - mu
    var = jnp.mean(zc * zc, axis=1, keepdims=True)
    return zc * lax.rsqrt(var + LN_EPS) * lng_ref[...] + lnb_ref[...]


def _outproj_ln_kernel(o_ref, w_ref, x_ref, mod_ref, lng_ref, lnb_ref, y_ref):
    y = jnp.dot(o_ref[0], w_ref[...], preferred_element_type=F32)
    y_ref[0] = _deepnorm(y, x_ref[0], mod_ref, lng_ref, lnb_ref)


def _outproj_ln(o, w_bf16, x, mod, ln_g, ln_b):
    b, s, d = x.shape
    width = o.shape[2]
    row = pl.BlockSpec((1, d), lambda bi, i: (0, 0))
    return pl.pallas_call(
        _outproj_ln_kernel,
        grid=(b, s // OUT_ROW_TILE),
        in_specs=[pl.BlockSpec((1, OUT_ROW_TILE, width), lambda bi, i: (bi, i, 0)),
                  pl.BlockSpec((width, d), lambda bi, i: (0, 0)),
                  pl.BlockSpec((1, OUT_ROW_TILE, d), lambda bi, i: (bi, i, 0)),
                  pl.BlockSpec((1, 3, d), lambda bi, i: (bi, 0, 0)),
                  row, row],
        out_specs=pl.BlockSpec((1, OUT_ROW_TILE, d), lambda bi, i: (bi, i, 0)),
        out_shape=jax.ShapeDtypeStruct((b, s, d), F32),
        compiler_params=pltpu.CompilerParams(vmem_limit_bytes=VMEM_LIMIT),
        name="outproj_ln",
    )(o, w_bf16, x, mod, ln_g.reshape(1, d), ln_b.reshape(1, d))


B_HEAD_PAD = LANES
B_QK_PAD = B_HEADS * B_HEAD_PAD
B_COL_KV = B_Q_RANK
B_COL_ROPE = B_Q_RANK + B_KV_RANK
B_COL_GATE = B_COL_ROPE + LANES
B_IN_PAD = B_COL_GATE + B_WIDTH
B_COL_CHUNK = 512


def _rms(x, g_row, eps):
    ms = jnp.mean(x * x, axis=1, keepdims=True)
    return x * lax.rsqrt(ms + eps) * g_row


def _b_inproj_kernel(o_ref, wo_ref, x0_ref, mod0_ref, lng_ref, lnb_ref, mod_ref, pos_ref, freq_ref,
                     consts_ref, w1_ref, qg_ref, wuq_ref, kvg_ref, wk_ref, wv_ref,
                     x_ref, q_ref, k_ref, v_ref, g_ref):
    i = pl.program_id(1)
    shift = mod_ref[0, 0:1, :]
    scale = mod_ref[0, 1:2, :]
    y = jnp.dot(o_ref[0], wo_ref[...], preferred_element_type=F32)
    x = _deepnorm(y, x0_ref[0], mod0_ref, lng_ref, lnb_ref)
    x_ref[0] = x
    u = (x * (1.0 + scale) + shift).astype(BF16)
    cos_t, sin_up, sin_dn = _rope_tables(pos_ref, i * (ROW_TILE // LANES), ROW_TILE // LANES,
                                         freq_ref, consts_ref)
    half = B_ROPE // 2
    qk_scale = (B_NOPE + B_ROPE) ** -0.5 * LOG2E

    lat = jnp.dot(u, w1_ref[:, 0:B_COL_GATE], preferred_element_type=F32)
    qn = _rms(lat[:, 0:B_Q_RANK], qg_ref[...], RMS_EPS).astype(BF16)
    kvn = _rms(lat[:, B_COL_KV:B_COL_ROPE], kvg_ref[...], RMS_EPS).astype(BF16)
    k_rope = _rope_apply(lat[:, B_COL_ROPE:B_COL_GATE], cos_t, sin_up, sin_dn, half)

    cos_q, up_q, dn_q = cos_t * qk_scale, sin_up * qk_scale, sin_dn * qk_scale
    for c0 in range(0, B_QK_PAD, B_COL_CHUNK):
        qc = jnp.dot(qn, wuq_ref[:, c0:c0 + B_COL_CHUNK], preferred_element_type=F32)
        kc = jnp.dot(kvn, wk_ref[:, c0:c0 + B_COL_CHUNK], preferred_element_type=F32)
        for g0 in range(0, B_COL_CHUNK, B_HEAD_PAD):
            cols = slice(c0 + g0, c0 + g0 + B_HEAD_PAD)
            q_ref[0, :, cols] = _rope_apply(qc[:, g0:g0 + B_HEAD_PAD], cos_q, up_q, dn_q,
                                            half).astype(BF16)
            k_ref[0, :, cols] = (kc[:, g0:g0 + B_HEAD_PAD] + k_rope).astype(BF16)
    v_ref[0] = jnp.dot(kvn, wv_ref[...], preferred_element_type=F32).astype(BF16)
    gate = jnp.dot(u, w1_ref[:, B_COL_GATE:B_IN_PAD], preferred_element_type=F32)
    g_ref[0] = _silu(gate).astype(BF16)


def _b_inproj(o, wo, x0, mod0, ln_g, ln_b, mod, pos3, rope, w1, qg, wuq, kvg, wk, wv):
    b, s, d = x0.shape
    freq, consts = rope

    def full(a):
        return pl.BlockSpec(a.shape, lambda bi, i: (0,) * a.ndim,
                            pipeline_mode=pl.Buffered(1))

    def rows(width):
        return pl.BlockSpec((1, ROW_TILE, width), lambda bi, i: (bi, i, 0))

    mod_spec = pl.BlockSpec((1, 3, d), lambda bi, i: (bi, 0, 0))
    ln_g, ln_b = ln_g.reshape(1, d), ln_b.reshape(1, d)
    return pl.pallas_call(
        _b_inproj_kernel,
        grid=(b, s // ROW_TILE),
        in_specs=[rows(o.shape[2]), full(wo), rows(d), mod_spec, full(ln_g), full(ln_b),
                  mod_spec,
                  pl.BlockSpec((1, s // LANES, LANES), lambda bi, i: (bi, 0, 0)),
                  full(freq), full(consts), full(w1), full(qg), full(wuq), full(kvg), full(wk),
                  full(wv)],
        out_specs=[rows(d), rows(B_QK_PAD), rows(B_QK_PAD), rows(B_WIDTH), rows(B_WIDTH)],
        out_shape=[jax.ShapeDtypeStruct((b, s, d), F32),
                   jax.ShapeDtypeStruct((b, s, B_QK_PAD), BF16),
                   jax.ShapeDtypeStruct((b, s, B_QK_PAD), BF16),
                   jax.ShapeDtypeStruct((b, s, B_WIDTH), BF16),
                   jax.ShapeDtypeStruct((b, s, B_WIDTH), BF16)],
        compiler_params=pltpu.CompilerParams(vmem_limit_bytes=VMEM_LIMIT),
        name="b_inproj",
    )(o, wo, x0, mod0, ln_g, ln_b, mod, pos3, freq, consts, w1, qg, wuq, kvg, wk, wv)


def _b_attn_kernel(q_ref, k_ref, v_ref, g_ref, o_ref, vt_ref, s_ref, p_ref):
    s = q_ref.shape[1]
    _store_transposed_values(v_ref, vt_ref)

    def finalize(qi, outs):
        rows = slice(qi * Q_TILE, (qi + 1) * Q_TILE)
        o_t = jnp.concatenate([acc * (1.0 / l) for l, acc in outs], axis=0)
        o_ref[0, rows, :] = (o_t.T * g_ref[0, rows, :].astype(F32)).astype(BF16)

    _attention_pipeline(
        s // Q_TILE, B_HEADS_PER_STEP,
        lambda qi, st: q_ref[0, qi * Q_TILE:(qi + 1) * Q_TILE,
                             st * B_HEAD_PAD:(st + 1) * B_HEAD_PAD],
        lambda st: (lambda r0, r1: k_ref[0, r0:r1, st * B_HEAD_PAD:(st + 1) * B_HEAD_PAD]),
        lambda st: (lambda c0, c1: vt_ref[st * B_VDIM:(st + 1) * B_VDIM, c0:c1]),
        s_ref, p_ref, finalize)


def _b_attention(q, k, v, g):
    b, s, _ = q.shape
    qk_spec = pl.BlockSpec((1, s, B_HEADS_PER_STEP * B_HEAD_PAD), lambda bi, h: (bi, 0, h))
    spec = pl.BlockSpec((1, s, B_HEADS_PER_STEP * B_VDIM), lambda bi, h: (bi, 0, h))
    return pl.pallas_call(
        _b_attn_kernel,
        grid=(b, B_HEADS // B_HEADS_PER_STEP),
        in_specs=[qk_spec, qk_spec, spec, spec],
        out_specs=spec,
        out_shape=jax.ShapeDtypeStruct((b, s, B_WIDTH), BF16),
        scratch_shapes=[pltpu.VMEM((B_HEADS_PER_STEP * B_VDIM, s), BF16),
                        pltpu.VMEM((2 * B_HEADS_PER_STEP, s, Q_TILE), F32),
                        pltpu.VMEM((2 * B_HEADS_PER_STEP, s, Q_TILE), BF16)],
        compiler_params=pltpu.CompilerParams(vmem_limit_bytes=VMEM_LIMIT),
        name="b_attention",
    )(q, k, v, g)


def _pad_rows8(rows):
    out = jnp.zeros((8, LANES), F32)
    for r, vec in enumerate(rows):
        out = out.at[r, :vec.shape[0]].set(vec.astype(F32))
    return out


def _b_layouts(w_in, w_uq, w_ukv):
    d = w_in.shape[0]
    rope_cols = jnp.zeros((d, LANES), F32).at[:, B_NOPE:B_NOPE + B_ROPE].set(
        w_in[:, B_COL_ROPE:B_COL_ROPE + B_ROPE])
    w1 = jnp.concatenate([w_in[:, :B_COL_ROPE], rope_cols, w_in[:, B_COL_ROPE + B_ROPE:]], axis=1)
    uq = w_uq.reshape(B_Q_RANK, B_HEADS, B_NOPE + B_ROPE)
    uq = jnp.pad(uq, ((0, 0), (0, 0), (0, B_HEAD_PAD - B_NOPE - B_ROPE)))
    ukv = w_ukv.reshape(B_KV_RANK, B_HEADS, B_NOPE + B_VDIM)
    wk = jnp.pad(ukv[:, :, :B_NOPE], ((0, 0), (0, 0), (0, B_HEAD_PAD - B_NOPE)))
    wv = ukv[:, :, B_NOPE:]
    return (w1.astype(BF16), uq.reshape(B_Q_RANK, B_QK_PAD).astype(BF16),
            wk.reshape(B_KV_RANK, B_QK_PAD).astype(BF16),
            wv.reshape(B_KV_RANK, B_WIDTH).astype(BF16))


def kernel(x, c, positions, ada_w, ada_b, ln_g, ln_b, a_w_in, a_lambda_q1, a_lambda_k1,
           a_lambda_q2, a_lambda_k2, a_subln_g, a_w_out, b_w_in, b_q_norm_g, b_w_uq,
           b_kv_norm_g, b_w_ukv, b_w_out):
    b, s, d = x.shape
    assert d == D_MODEL and s % ROW_TILE == 0 and s % Q_TILE == 0 and Q_TILE == KV_TILE
    mod = _modulation(c, ada_w, ada_b).reshape(DEPTH, b, 3, d)
    pos3 = positions.reshape(b, s // LANES, LANES)

    lambda_init = 0.8 - 0.6 * math.exp(-0.3 * 0)
    q, k, v, g = _a_inproj(x, mod[0], pos3, _rope_consts(A_ROT, A_HEAD_DIM, 0),
                           a_w_in[0].astype(BF16))
    lam_rows = _pad_rows8([a_lambda_q1[0], a_lambda_k1[0], a_lambda_q2[0], a_lambda_k2[0]])
    o = _a_attention(q, k, v, g, lam_rows, _pad_rows8([a_subln_g[0]]), lambda_init)

    w1, wuq, wk, wv = _b_layouts(b_w_in[0], b_w_uq[0], b_w_ukv[0])
    x, q, k, v, g = _b_inproj(o, a_w_out[0].astype(BF16), x, mod[0], ln_g[0], ln_b[0], mod[1],
                              pos3, _rope_consts(B_ROPE, B_HEAD_PAD, B_NOPE), w1,
                              b_q_norm_g[0].reshape(1, B_Q_RANK), wuq,
                              b_kv_norm_g[0].reshape(1, B_KV_RANK), wk, wv)
    o = _b_attention(q, k, v, g)
    return _outproj_ln(o, b_w_out[0].astype(BF16), x, mod[1], ln_g[1], ln_b[1])
```

```python
import functools
import math

import jax
import jax.numpy as jnp
import numpy as np
from jax import lax
from jax.experimental import pallas as pl
from jax.experimental.pallas import tpu as pltpu

F32 = jnp.float32
BF16 = jnp.bfloat16

D_MODEL = 1024
CHUNK = 64
ROPE_THETA = 500000.0

A_HEADS = 8
A_HEAD_DIM = 64
A_ROT = A_HEAD_DIM // 4
A_WIDTH = A_HEADS * 2 * A_HEAD_DIM

B_HEADS = 16
B_NOPE = 64
B_ROPE = 32
B_VDIM = 64
B_Q_RANK = 512
B_KV_RANK = 256
B_WIDTH = B_HEADS * B_VDIM

DEPTH = 2
DEEPNORM_ALPHA = (2.0 * DEPTH) ** 0.25
LN_EPS = 1e-5
RMS_EPS = 1e-6
SUBLN_EPS = 1e-5
NEG_INF = -1e30

LANES = 128
FREQ_ROWS = 16
ROW_TILE = 512
OUT_ROW_TILE = 1024
Q_TILE = 256
KV_TILE = 256
SCORE_ROWS = 512
A_HEADS_PER_STEP = 2
B_HEADS_PER_STEP = 4
LOG2E = math.log2(math.e)
V7X_VMEM_BYTES = 64 * 1024 * 1024
VMEM_LIMIT = V7X_VMEM_BYTES // 4 * 3

_NT = (((1,), (1,)), ((), ()))


def _silu(x):
    return x * (1.0 / (1.0 + jnp.exp(-x)))


def _mod_kernel(c_ref, w_ref, b_ref, o_ref):
    ca = _silu(c_ref[...]).astype(BF16)
    o_ref[0] = jnp.dot(ca, w_ref[0].astype(BF16), preferred_element_type=F32) + b_ref[0]


def _modulation(c, ada_w, ada_b):
    depth, d, n3 = ada_w.shape
    b = c.shape[0]
    tn = 1024
    return pl.pallas_call(
        _mod_kernel,
        grid=(depth, n3 // tn),
        in_specs=[pl.BlockSpec((b, d), lambda i, j: (0, 0)),
                  pl.BlockSpec((1, d, tn), lambda i, j: (i, 0, j)),
                  pl.BlockSpec((1, 1, tn), lambda i, j: (i, 0, j))],
        out_specs=pl.BlockSpec((1, b, tn), lambda i, j: (i, 0, j)),
        out_shape=jax.ShapeDtypeStruct((depth, b, n3), F32),
        compiler_params=pltpu.CompilerParams(vmem_limit_bytes=VMEM_LIMIT),
        name="modulation",
    )(c, ada_w, ada_b.reshape(depth, 1, n3))


def _rope_tables(pos_ref, first_chunk, n_chunks, freq_ref, consts_ref):
    freq = freq_ref[...]
    cs, sn = [], []
    for r in range(n_chunks):
        pos = pos_ref[0, pl.ds(first_chunk + r, 1), :].astype(F32)
        ang = freq * pos
        reps = LANES // FREQ_ROWS
        cs.append(jnp.concatenate([jnp.cos(ang)] * reps, axis=0).T)
        sn.append(jnp.concatenate([jnp.sin(ang)] * reps, axis=0).T)
    cos_t = jnp.where(consts_ref[0:1, :] > 0.0, jnp.concatenate(cs, axis=0), 1.0)
    sin_t = jnp.concatenate(sn, axis=0)
    return cos_t, sin_t * consts_ref[1:2, :], sin_t * consts_ref[2:3, :]


def _rope_apply(a, cos_t, sin_up, sin_dn, half):
    return (a * cos_t + pltpu.roll(a, LANES - half, 1) * sin_up
            + pltpu.roll(a, half, 1) * sin_dn)


def _rope_consts(rot_dim, group, offset):
    half = rot_dim // 2
    assert FREQ_ROWS % half == 0 and offset % half == 0 and group % half == 0
    inv_freq = ROPE_THETA ** (-jnp.arange(0, rot_dim, 2, dtype=F32) / rot_dim)
    freq_rows = jnp.broadcast_to(jnp.tile(inv_freq, FREQ_ROWS // half)[:, None],
                                 (FREQ_ROWS, LANES))
    d = np.arange(LANES) % group - offset
    rotated = (d >= 0) & (d < rot_dim)
    first = rotated & (d < half)
    second = rotated & (d >= half)
    consts = np.zeros((8, LANES), np.float32)
    consts[0], consts[1], consts[2] = rotated, -first.astype(np.float32), second
    return freq_rows, jnp.asarray(consts)


def _a_inproj_kernel(x_ref, mod_ref, pos_ref, freq_ref, consts_ref, w_ref, q_ref, k_ref, v_ref,
                     g_ref):
    i = pl.program_id(1)
    shift = mod_ref[0, 0:1, :]
    scale = mod_ref[0, 1:2, :]
    u = (x_ref[0] * (1.0 + scale) + shift).astype(BF16)
    cos_t, sin_up, sin_dn = _rope_tables(pos_ref, i * (ROW_TILE // LANES), ROW_TILE // LANES,
                                         freq_ref, consts_ref)
    half = A_ROT // 2
    qk_scale = A_HEAD_DIM ** -0.5 * LOG2E

    def rope_store(col0, o_ref, mult):
        acc = jnp.dot(u, w_ref[:, col0:col0 + A_WIDTH], preferred_element_type=F32)
        for g in range(A_WIDTH // LANES):
            a = acc[:, g * LANES:(g + 1) * LANES]
            r = _rope_apply(a, cos_t, sin_up, sin_dn, half)
            if mult != 1.0:
                r = r * mult
            o_ref[0, :, g * LANES:(g + 1) * LANES] = r.astype(BF16)

    rope_store(0, q_ref, qk_scale)
    rope_store(A_WIDTH, k_ref, 1.0)
    v_ref[0] = jnp.dot(u, w_ref[:, 2 * A_WIDTH:3 * A_WIDTH],
                       preferred_element_type=F32).astype(BF16)
    gate = jnp.dot(u, w_ref[:, 3 * A_WIDTH:4 * A_WIDTH], preferred_element_type=F32)
    g_ref[0] = _silu(gate).astype(BF16)


def _a_inproj(x, mod, pos3, rope, w_bf16):
    b, s, d = x.shape
    freq, consts = rope
    n = w_bf16.shape[1]
    out = jax.ShapeDtypeStruct((b, s, A_WIDTH), BF16)
    ospec = pl.BlockSpec((1, ROW_TILE, A_WIDTH), lambda bi, i: (bi, i, 0))
    return pl.pallas_call(
        _a_inproj_kernel,
        grid=(b, s // ROW_TILE),
        in_specs=[pl.BlockSpec((1, ROW_TILE, d), lambda bi, i: (bi, i, 0)),
                  pl.BlockSpec((1, 3, d), lambda bi, i: (bi, 0, 0)),
                  pl.BlockSpec((1, s // LANES, LANES), lambda bi, i: (bi, 0, 0)),
                  pl.BlockSpec((FREQ_ROWS, LANES), lambda bi, i: (0, 0)),
                  pl.BlockSpec((8, LANES), lambda bi, i: (0, 0)),
                  pl.BlockSpec((d, n), lambda bi, i: (0, 0))],
        out_specs=[ospec, ospec, ospec, ospec],
        out_shape=[out, out, out, out],
        compiler_params=pltpu.CompilerParams(vmem_limit_bytes=VMEM_LIMIT),
        name="a_inproj",
    )(x, mod, pos3, freq, consts, w_bf16)


def _chunk_mask():
    krow = lax.broadcasted_iota(jnp.int32, (KV_TILE, Q_TILE), 0) // CHUNK
    qcol = lax.broadcasted_iota(jnp.int32, (KV_TILE, Q_TILE), 1) // CHUNK
    return krow <= qcol


def _score_items(q, k_rows, kv_len, s_ref, mask, out):
    diag = kv_len - KV_TILE

    def boundary():
        s_d = lax.dot_general(k_rows(diag, kv_len), q, _NT, preferred_element_type=F32)
        s_d = jnp.where(mask, s_d, NEG_INF)
        s_ref[diag:kv_len, :] = s_d
        out["m"] = jnp.max(s_d, axis=0, keepdims=True)

    def interior(r0, r1):
        s_o = lax.dot_general(k_rows(r0, r1), q, _NT, preferred_element_type=F32)
        s_ref[r0:r1, :] = s_o
        out["m"] = jnp.maximum(out["m"], jnp.max(s_o, axis=0, keepdims=True))

    items = [boundary]
    for r0 in range(0, diag, SCORE_ROWS):
        items.append(functools.partial(interior, r0, min(r0 + SCORE_ROWS, diag)))
    return items


def _exp_items(kv_len, s_ref, p_ref, state):
    def block(r0):
        p = jnp.exp2(s_ref[r0:r0 + KV_TILE, :] - state["m"])
        part = jnp.sum(p, axis=0, keepdims=True)
        state["l"] = part if "l" not in state else state["l"] + part
        p_ref[r0:r0 + KV_TILE, :] = p.astype(BF16)

    return [functools.partial(block, r0) for r0 in range(0, kv_len, KV_TILE)]


def _emit_interleaved(item_lists):
    item_lists = [items for items in item_lists if items]
    done = [0] * len(item_lists)
    for _ in range(sum(len(items) for items in item_lists)):
        i = min((i for i in range(len(item_lists)) if done[i] < len(item_lists[i])),
                key=lambda i: (done[i] + 0.5) / len(item_lists[i]))
        item_lists[i][done[i]]()
        done[i] += 1


def _attention_pipeline(n_q, n_streams, q_of, k_rows_of, vt_cols_of, s_ref, p_ref, finalize):
    mask = _chunk_mask()
    state = {}

    def slot(qi, st):
        return (qi % 2) * n_streams + st

    def pv_item(qi, st):
        kv_len = (qi + 1) * Q_TILE
        state[qi, st]["acc"] = jnp.dot(vt_cols_of(st)(0, kv_len),
                                       p_ref[slot(qi, st), 0:kv_len, :],
                                       preferred_element_type=F32)

    for t in range(n_q + 2):
        scores, exps, pvs = [], [], []
        for st in range(n_streams):
            if t < n_q:
                state[t, st] = {}
                scores.append(_score_items(q_of(t, st), k_rows_of(st), (t + 1) * Q_TILE,
                                           s_ref.at[slot(t, st)], mask, state[t, st]))
            if 0 <= t - 1 < n_q:
                exps.append(_exp_items(t * Q_TILE, s_ref.at[slot(t - 1, st)],
                                       p_ref.at[slot(t - 1, st)], state[t - 1, st]))
            if 0 <= t - 2 < n_q:
                pvs.append([functools.partial(pv_item, t - 2, st)])
        _emit_interleaved(scores + exps + pvs)
        if 0 <= t - 2 < n_q:
            done = [state.pop((t - 2, st)) for st in range(n_streams)]
            finalize(t - 2, [(d["l"], d["acc"]) for d in done])


def _store_transposed_values(v_ref, vt_ref):
    for r0 in range(0, v_ref.shape[1], KV_TILE):
        blk = v_ref[0, r0:r0 + KV_TILE, :].astype(F32)
        vt_ref[:, r0:r0 + KV_TILE] = blk.T.astype(BF16)


def _a_attn_kernel(lambda_init, q_ref, k_ref, v_ref, g_ref, lam_ref, subg_ref, o_ref,
                   vt_ref, s_ref, p_ref):
    s = q_ref.shape[1]
    first_map = lax.broadcasted_iota(jnp.int32, (Q_TILE, LANES), 1) < A_HEAD_DIM
    dv = 2 * A_HEAD_DIM
    _store_transposed_values(v_ref, vt_ref)

    def map_query(qi, st):
        hh, mp = divmod(st, 2)
        q = q_ref[0, qi * Q_TILE:(qi + 1) * Q_TILE, hh * LANES:(hh + 1) * LANES].astype(F32)
        return jnp.where(first_map == (mp == 0), q, 0.0).astype(BF16)

    lam = (jnp.exp(jnp.sum(lam_ref[0:1, :] * lam_ref[1:2, :], axis=1, keepdims=True))
           - jnp.exp(jnp.sum(lam_ref[2:3, :] * lam_ref[3:4, :], axis=1, keepdims=True))
           + lambda_init)
    out_gain = subg_ref[0:1, :] * (1.0 - lambda_init)

    def finalize(qi, outs):
        rows = slice(qi * Q_TILE, (qi + 1) * Q_TILE)
        for hh in range(A_HEADS_PER_STEP):
            cols = slice(hh * LANES, (hh + 1) * LANES)
            (l1, acc1), (l2, acc2) = outs[2 * hh:2 * hh + 2]
            o_t = acc1 * (1.0 / l1) - acc2 * (lam / l2)
            ms = jnp.mean(o_t * o_t, axis=0, keepdims=True)
            o_t = o_t * lax.rsqrt(ms + SUBLN_EPS)
            o = o_t.T * out_gain * g_ref[0, rows, cols].astype(F32)
            o_ref[0, rows, cols] = o.astype(BF16)

    _attention_pipeline(
        s // Q_TILE, 2 * A_HEADS_PER_STEP, map_query,
        lambda st: (lambda r0, r1: k_ref[0, r0:r1, (st // 2) * LANES:(st // 2 + 1) * LANES]),
        lambda st: (lambda c0, c1: vt_ref[(st // 2) * dv:(st // 2 + 1) * dv, c0:c1]),
        s_ref, p_ref, finalize)


def _a_attention(q, k, v, g, lam_rows, subg_rows, lambda_init):
    b, s, _ = q.shape
    n_streams = 2 * A_HEADS_PER_STEP
    spec = pl.BlockSpec((1, s, A_HEADS_PER_STEP * LANES), lambda bi, h: (bi, 0, h))
    small = pl.BlockSpec((8, LANES), lambda bi, h: (0, 0))
    return pl.pallas_call(
        functools.partial(_a_attn_kernel, lambda_init),
        grid=(b, A_HEADS // A_HEADS_PER_STEP),
        in_specs=[spec, spec, spec, spec, small, small],
        out_specs=spec,
        out_shape=jax.ShapeDtypeStruct((b, s, A_WIDTH), BF16),
        scratch_shapes=[pltpu.VMEM((A_HEADS_PER_STEP * LANES, s), BF16),
                        pltpu.VMEM((2 * n_streams, s, Q_TILE), F32),
                        pltpu.VMEM((2 * n_streams, s, Q_TILE), BF16)],
        compiler_params=pltpu.CompilerParams(vmem_limit_bytes=VMEM_LIMIT),
        name="a_attention",
    )(q, k, v, g, lam_rows, subg_rows)


def _deepnorm(y, x, mod_ref, lng_ref, lnb_ref):
    gate = mod_ref[0, 2:3, :]
    z = DEEPNORM_ALPHA * x + gate * y
    mu = jnp.mean(z, axis=1, keepdims=True)
    zc = z - mu
    var = jnp.mean(zc * zc, axis=1, keepdims=True)
    return zc * lax.rsqrt(var + LN_EPS) * lng_ref[...] + lnb_ref[...]


def _outproj_ln_kernel(o_ref, w_ref, x_ref, mod_ref, lng_ref, lnb_ref, y_ref):
    y = jnp.dot(o_ref[0], w_ref[...], preferred_element_type=F32)
    y_ref[0] = _deepnorm(y, x_ref[0], mod_ref, lng_ref, lnb_ref)


def _outproj_ln(o, w_bf16, x, mod, ln_g, ln_b):
    b, s, d = x.shape
    width = o.shape[2]
    row = pl.BlockSpec((1, d), lambda bi, i: (0, 0))
    return pl.pallas_call(
        _outproj_ln_kernel,
        grid=(b, s // OUT_ROW_TILE),
        in_specs=[pl.BlockSpec((1, OUT_ROW_TILE, width), lambda bi, i: (bi, i, 0)),
                  pl.BlockSpec((width, d), lambda bi, i: (0, 0)),
                  pl.BlockSpec((1, OUT_ROW_TILE, d), lambda bi, i: (bi, i, 0)),
                  pl.BlockSpec((1, 3, d), lambda bi, i: (bi, 0, 0)),
                  row, row],
        out_specs=pl.BlockSpec((1, OUT_ROW_TILE, d), lambda bi, i: (bi, i, 0)),
        out_shape=jax.ShapeDtypeStruct((b, s, d), F32),
        compiler_params=pltpu.CompilerParams(vmem_limit_bytes=VMEM_LIMIT),
        name="outproj_ln",
    )(o, w_bf16, x, mod, ln_g.reshape(1, d), ln_b.reshape(1, d))


B_HEAD_PAD = LANES
B_QK_PAD = B_HEADS * B_HEAD_PAD
B_COL_KV = B_Q_RANK
B_COL_ROPE = B_Q_RANK + B_KV_RANK
B_COL_GATE = B_COL_ROPE + LANES
B_IN_PAD = B_COL_GATE + B_WIDTH
B_COL_CHUNK = 512


def _rms(x, g_row, eps):
    ms = jnp.mean(x * x, axis=1, keepdims=True)
    return x * lax.rsqrt(ms + eps) * g_row


def _b_inproj_kernel(o_ref, wo_ref, x0_ref, mod0_ref, lng_ref, lnb_ref, mod_ref, pos_ref, freq_ref,
                     consts_ref, w1_ref, qg_ref, wuq_ref, kvg_ref, wk_ref, wv_ref,
                     x_ref, q_ref, k_ref, v_ref, g_ref):
    i = pl.program_id(1)
    shift = mod_ref[0, 0:1, :]
    scale = mod_ref[0, 1:2, :]
    y = jnp.dot(o_ref[0], wo_ref[...], preferred_element_type=F32)
    x = _deepnorm(y, x0_ref[0], mod0_ref, lng_ref, lnb_ref)
    x_ref[0] = x
    u = (x * (1.0 + scale) + shift).astype(BF16)
    cos_t, sin_up, sin_dn = _rope_tables(pos_ref, i * (ROW_TILE // LANES), ROW_TILE // LANES,
                                         freq_ref, consts_ref)
    half = B_ROPE // 2
    qk_scale = (B_NOPE + B_ROPE) ** -0.5 * LOG2E

    lat = jnp.dot(u, w1_ref[:, 0:B_COL_GATE], preferred_element_type=F32)
    qn = _rms(lat[:, 0:B_Q_RANK], qg_ref[...], RMS_EPS).astype(BF16)
    kvn = _rms(lat[:, B_COL_KV:B_COL_ROPE], kvg_ref[...], RMS_EPS).astype(BF16)
    k_rope = _rope_apply(lat[:, B_COL_ROPE:B_COL_GATE], cos_t, sin_up, sin_dn, half)

    cos_q, up_q, dn_q = cos_t * qk_scale, sin_up * qk_scale, sin_dn * qk_scale
    for c0 in range(0, B_QK_PAD, B_COL_CHUNK):
        qc = jnp.dot(qn, wuq_ref[:, c0:c0 + B_COL_CHUNK], preferred_element_type=F32)
        kc = jnp.dot(kvn, wk_ref[:, c0:c0 + B_COL_CHUNK], preferred_element_type=F32)
        for g0 in range(0, B_COL_CHUNK, B_HEAD_PAD):
            cols = slice(c0 + g0, c0 + g0 + B_HEAD_PAD)
            q_ref[0, :, cols] = _rope_apply(qc[:, g0:g0 + B_HEAD_PAD], cos_q, up_q, dn_q,
                                            half).astype(BF16)
            k_ref[0, :, cols] = (kc[:, g0:g0 + B_HEAD_PAD] + k_rope).astype(BF16)
    v_ref[0] = jnp.dot(kvn, wv_ref[...], preferred_element_type=F32).astype(BF16)
    gate = jnp.dot(u, w1_ref[:, B_COL_GATE:B_IN_PAD], preferred_element_type=F32)
    g_ref[0] = _silu(gate).astype(BF16)


def _b_inproj(o, wo, x0, mod0, ln_g, ln_b, mod, pos3, rope, w1, qg, wuq, kvg, wk, wv):
    b, s, d = x0.shape
    freq, consts = rope

    def full(a):
        return pl.BlockSpec(a.shape, lambda bi, i: (0,) * a.ndim,
                            pipeline_mode=pl.Buffered(1))

    def rows(width):
        return pl.BlockSpec((1, ROW_TILE, width), lambda bi, i: (bi, i, 0))

    mod_spec = pl.BlockSpec((1, 3, d), lambda bi, i: (bi, 0, 0))
    ln_g, ln_b = ln_g.reshape(1, d), ln_b.reshape(1, d)
    return pl.pallas_call(
        _b_inproj_kernel,
        grid=(b, s // ROW_TILE),
        in_specs=[rows(o.shape[2]), full(wo), rows(d), mod_spec, full(ln_g), full(ln_b),
                  mod_spec,
                  pl.BlockSpec((1, s // LANES, LANES), lambda bi, i: (bi, 0, 0)),
                  full(freq), full(consts), full(w1), full(qg), full(wuq), full(kvg), full(wk),
                  full(wv)],
        out_specs=[rows(d), rows(B_QK_PAD), rows(B_QK_PAD), rows(B_WIDTH), rows(B_WIDTH)],
        out_shape=[jax.ShapeDtypeStruct((b, s, d), F32),
                   jax.ShapeDtypeStruct((b, s, B_QK_PAD), BF16),
                   jax.ShapeDtypeStruct((b, s, B_QK_PAD), BF16),
                   jax.ShapeDtypeStruct((b, s, B_WIDTH), BF16),
                   jax.ShapeDtypeStruct((b, s, B_WIDTH), BF16)],
        compiler_params=pltpu.CompilerParams(vmem_limit_bytes=VMEM_LIMIT),
        name="b_inproj",
    )(o, wo, x0, mod0, ln_g, ln_b, mod, pos3, freq, consts, w1, qg, wuq, kvg, wk, wv)


def _b_attn_kernel(q_ref, k_ref, v_ref, g_ref, o_ref, vt_ref, s_ref, p_ref):
    s = q_ref.shape[1]
    _store_transposed_values(v_ref, vt_ref)

    def finalize(qi, outs):
        rows = slice(qi * Q_TILE, (qi + 1) * Q_TILE)
        o_t = jnp.concatenate([acc * (1.0 / l) for l, acc in outs], axis=0)
        o_ref[0, rows, :] = (o_t.T * g_ref[0, rows, :].astype(F32)).astype(BF16)

    _attention_pipeline(
        s // Q_TILE, B_HEADS_PER_STEP,
        lambda qi, st: q_ref[0, qi * Q_TILE:(qi + 1) * Q_TILE,
                             st * B_HEAD_PAD:(st + 1) * B_HEAD_PAD],
        lambda st: (lambda r0, r1: k_ref[0, r0:r1, st * B_HEAD_PAD:(st + 1) * B_HEAD_PAD]),
        lambda st: (lambda c0, c1: vt_ref[st * B_VDIM:(st + 1) * B_VDIM, c0:c1]),
        s_ref, p_ref, finalize)


def _b_attention(q, k, v, g):
    b, s, _ = q.shape
    qk_spec = pl.BlockSpec((1, s, B_HEADS_PER_STEP * B_HEAD_PAD), lambda bi, h: (bi, 0, h))
    spec = pl.BlockSpec((1, s, B_HEADS_PER_STEP * B_VDIM), lambda bi, h: (bi, 0, h))
    return pl.pallas_call(
        _b_attn_kernel,
        grid=(b, B_HEADS // B_HEADS_PER_STEP),
        in_specs=[qk_spec, qk_spec, spec, spec],
        out_specs=spec,
        out_shape=jax.ShapeDtypeStruct((b, s, B_WIDTH), BF16),
        scratch_shapes=[pltpu.VMEM((B_HEADS_PER_STEP * B_VDIM, s), BF16),
                        pltpu.VMEM((2 * B_HEADS_PER_STEP, s, Q_TILE), F32),
                        pltpu.VMEM((2 * B_HEADS_PER_STEP, s, Q_TILE), BF16)],
        compiler_params=pltpu.CompilerParams(vmem_limit_bytes=VMEM_LIMIT),
        name="b_attention",
    )(q, k, v, g)


def _pad_rows8(rows):
    out = jnp.zeros((8, LANES), F32)
    for r, vec in enumerate(rows):
        out = out.at[r, :vec.shape[0]].set(vec.astype(F32))
    return out


def _b_layouts(w_in, w_uq, w_ukv):
    d = w_in.shape[0]
    rope_cols = jnp.zeros((d, LANES), F32).at[:, B_NOPE:B_NOPE + B_ROPE].set(
        w_in[:, B_COL_ROPE:B_COL_ROPE + B_ROPE])
    w1 = jnp.concatenate([w_in[:, :B_COL_ROPE], rope_cols, w_in[:, B_COL_ROPE + B_ROPE:]], axis=1)
    uq = w_uq.reshape(B_Q_RANK, B_HEADS, B_NOPE + B_ROPE)
    uq = jnp.pad(uq, ((0, 0), (0, 0), (0, B_HEAD_PAD - B_NOPE - B_ROPE)))
    ukv = w_ukv.reshape(B_KV_RANK, B_HEADS, B_NOPE + B_VDIM)
    wk = jnp.pad(ukv[:, :, :B_NOPE], ((0, 0), (0, 0), (0, B_HEAD_PAD - B_NOPE)))
    wv = ukv[:, :, B_NOPE:]
    return (w1.astype(BF16), uq.reshape(B_Q_RANK, B_QK_PAD).astype(BF16),
            wk.reshape(B_KV_RANK, B_QK_PAD).astype(BF16),
            wv.reshape(B_KV_RANK, B_WIDTH).astype(BF16))


def kernel(x, c, positions, ada_w, ada_b, ln_g, ln_b, a_w_in, a_lambda_q1, a_lambda_k1,
           a_lambda_q2, a_lambda_k2, a_subln_g, a_w_out, b_w_in, b_q_norm_g, b_w_uq,
           b_kv_norm_g, b_w_ukv, b_w_out):
    b, s, d = x.shape
    assert d == D_MODEL and s % ROW_TILE == 0 and s % Q_TILE == 0 and Q_TILE == KV_TILE
    mod = _modulation(c, ada_w, ada_b).reshape(DEPTH, b, 3, d)
    pos3 = positions.reshape(b, s // LANES, LANES)

    lambda_init = 0.8 - 0.6 * math.exp(-0.3 * 0)
    q, k, v, g = _a_inproj(x, mod[0], pos3, _rope_consts(A_ROT, A_HEAD_DIM, 0),
                           a_w_in[0].astype(BF16))
    lam_rows = _pad_rows8([a_lambda_q1[0], a_lambda_k1[0], a_lambda_q2[0], a_lambda_k2[0]])
    o = _a_attention(q, k, v, g, lam_rows, _pad_rows8([a_subln_g[0]]), lambda_init)

    w1, wuq, wk, wv = _b_layouts(b_w_in[0], b_w_uq[0], b_w_ukv[0])
    x, q, k, v, g = _b_inproj(o, a_w_out[0].astype(BF16), x, mod[0], ln_g[0], ln_b[0], mod[1],
                              pos3, _rope_consts(B_ROPE, B_HEAD_PAD, B_NOPE), w1,
                              b_q_norm_g[0].reshape(1, B_Q_RANK), wuq,
                              b_kv_norm_g[0].reshape(1, B_KV_RANK), wk, wv)
    o = _b_attention(q, k, v, g)
    return _outproj_ln(o, b_w_out[0].astype(BF16), x, mod[1], ln_g[1], ln_b[1])
```

```python
import functools
import math

import jax
import jax.numpy as jnp
import numpy as np
from jax import lax
from jax.experimental import pallas as pl
from jax.experimental.pallas import tpu as pltpu

F32 = jnp.float32
BF16 = jnp.bfloat16

D_MODEL = 1024
CHUNK = 64
ROPE_THETA = 500000.0

A_HEADS = 8
A_HEAD_DIM = 64
A_ROT = A_HEAD_DIM // 4
A_WIDTH = A_HEADS * 2 * A_HEAD_DIM

B_HEADS = 16
B_NOPE = 64
B_ROPE = 32
B_VDIM = 64
B_Q_RANK = 512
B_KV_RANK = 256
B_WIDTH = B_HEADS * B_VDIM

DEPTH = 2
DEEPNORM_ALPHA = (2.0 * DEPTH) ** 0.25
LN_EPS = 1e-5
RMS_EPS = 1e-6
SUBLN_EPS = 1e-5
NEG_INF = -1e30

LANES = 128
FREQ_ROWS = 16
ROW_TILE = 512
OUT_ROW_TILE = 1024
Q_TILE = 256
KV_TILE = 256
SCORE_ROWS = 512
A_HEADS_PER_STEP = 2
B_HEADS_PER_STEP = 4
LOG2E = math.log2(math.e)
V7X_VMEM_BYTES = 64 * 1024 * 1024
VMEM_LIMIT = V7X_VMEM_BYTES // 4 * 3

_NT = (((1,), (1,)), ((), ()))


def _silu(x):
    return x * (1.0 / (1.0 + jnp.exp(-x)))


def _mod_kernel(c_ref, w_ref, b_ref, o_ref):
    ca = _silu(c_ref[...]).astype(BF16)
    o_ref[0] = jnp.dot(ca, w_ref[0].astype(BF16), preferred_element_type=F32) + b_ref[0]


def _modulation(c, ada_w, ada_b):
    depth, d, n3 = ada_w.shape
    b = c.shape[0]
    tn = 1024
    return pl.pallas_call(
        _mod_kernel,
        grid=(depth, n3 // tn),
        in_specs=[pl.BlockSpec((b, d), lambda i, j: (0, 0)),
                  pl.BlockSpec((1, d, tn), lambda i, j: (i, 0, j)),
                  pl.BlockSpec((1, 1, tn), lambda i, j: (i, 0, j))],
        out_specs=pl.BlockSpec((1, b, tn), lambda i, j: (i, 0, j)),
        out_shape=jax.ShapeDtypeStruct((depth, b, n3), F32),
        compiler_params=pltpu.CompilerParams(vmem_limit_bytes=VMEM_LIMIT),
        name="modulation",
    )(c, ada_w, ada_b.reshape(depth, 1, n3))


def _rope_tables(pos_ref, first_chunk, n_chunks, freq_ref, consts_ref):
    freq = freq_ref[...]
    cs, sn = [], []
    for r in range(n_chunks):
        pos = pos_ref[0, pl.ds(first_chunk + r, 1), :].astype(F32)
        ang = freq * pos
        reps = LANES // FREQ_ROWS
        cs.append(jnp.concatenate([jnp.cos(ang)] * reps, axis=0).T)
        sn.append(jnp.concatenate([jnp.sin(ang)] * reps, axis=0).T)
    cos_t = jnp.where(consts_ref[0:1, :] > 0.0, jnp.concatenate(cs, axis=0), 1.0)
    sin_t = jnp.concatenate(sn, axis=0)
    return cos_t, sin_t * consts_ref[1:2, :], sin_t * consts_ref[2:3, :]


def _rope_apply(a, cos_t, sin_up, sin_dn, half):
    return (a * cos_t + pltpu.roll(a, LANES - half, 1) * sin_up
            + pltpu.roll(a, half, 1) * sin_dn)


def _rope_consts(rot_dim, group, offset):
    half = rot_dim // 2
    assert FREQ_ROWS % half == 0 and offset % half == 0 and group % half == 0
    inv_freq = ROPE_THETA ** (-jnp.arange(0, rot_dim, 2, dtype=F32) / rot_dim)
    freq_rows = jnp.broadcast_to(jnp.tile(inv_freq, FREQ_ROWS // half)[:, None],
                                 (FREQ_ROWS, LANES))
    d = np.arange(LANES) % group - offset
    rotated = (d >= 0) & (d < rot_dim)
    first = rotated & (d < half)
    second = rotated & (d >= half)
    consts = np.zeros((8, LANES), np.float32)
    consts[0], consts[1], consts[2] = rotated, -first.astype(np.float32), second
    return freq_rows, jnp.asarray(consts)


def _a_inproj_kernel(x_ref, mod_ref, pos_ref, freq_ref, consts_ref, w_ref, q_ref, k_ref, v_ref,
                     g_ref):
    i = pl.program_id(1)
    shift = mod_ref[0, 0:1, :]
    scale = mod_ref[0, 1:2, :]
    u = (x_ref[0] * (1.0 + scale) + shift).astype(BF16)
    cos_t, sin_up, sin_dn = _rope_tables(pos_ref, i * (ROW_TILE // LANES), ROW_TILE // LANES,
                                         freq_ref, consts_ref)
    half = A_ROT // 2
    qk_scale = A_HEAD_DIM ** -0.5 * LOG2E

    def rope_store(col0, o_ref, mult):
        acc = jnp.dot(u, w_ref[:, col0:col0 + A_WIDTH], preferred_element_type=F32)
        for g in range(A_WIDTH // LANES):
            a = acc[:, g * LANES:(g + 1) * LANES]
            r = _rope_apply(a, cos_t, sin_up, sin_dn, half)
            if mult != 1.0:
                r = r * mult
            o_ref[0, :, g * LANES:(g + 1) * LANES] = r.astype(BF16)

    rope_store(0, q_ref, qk_scale)
    rope_store(A_WIDTH, k_ref, 1.0)
    v_ref[0] = jnp.dot(u, w_ref[:, 2 * A_WIDTH:3 * A_WIDTH],
                       preferred_element_type=F32).astype(BF16)
    gate = jnp.dot(u, w_ref[:, 3 * A_WIDTH:4 * A_WIDTH], preferred_element_type=F32)
    g_ref[0] = _silu(gate).astype(BF16)


def _a_inproj(x, mod, pos3, rope, w_bf16):
    b, s, d = x.shape
    freq, consts = rope
    n = w_bf16.shape[1]
    out = jax.ShapeDtypeStruct((b, s, A_WIDTH), BF16)
    ospec = pl.BlockSpec((1, ROW_TILE, A_WIDTH), lambda bi, i: (bi, i, 0))
    return pl.pallas_call(
        _a_inproj_kernel,
        grid=(b, s // ROW_TILE),
        in_specs=[pl.BlockSpec((1, ROW_TILE, d), lambda bi, i: (bi, i, 0)),
                  pl.BlockSpec((1, 3, d), lambda bi, i: (bi, 0, 0)),
                  pl.BlockSpec((1, s // LANES, LANES), lambda bi, i: (bi, 0, 0)),
                  pl.BlockSpec((FREQ_ROWS, LANES), lambda bi, i: (0, 0)),
                  pl.BlockSpec((8, LANES), lambda bi, i: (0, 0)),
                  pl.BlockSpec((d, n), lambda bi, i: (0, 0))],
        out_specs=[ospec, ospec, ospec, ospec],
        out_shape=[out, out, out, out],
        compiler_params=pltpu.CompilerParams(vmem_limit_bytes=VMEM_LIMIT),
        name="a_inproj",
    )(x, mod, pos3, freq, consts, w_bf16)


def _chunk_mask():
    krow = lax.broadcasted_iota(jnp.int32, (KV_TILE, Q_TILE), 0) // CHUNK
    qcol = lax.broadcasted_iota(jnp.int32, (KV_TILE, Q_TILE), 1) // CHUNK
    return krow <= qcol


def _score_items(q, k_rows, kv_len, s_ref, mask, out):
    diag = kv_len - KV_TILE

    def boundary():
        s_d = lax.dot_general(k_rows(diag, kv_len), q, _NT, preferred_element_type=F32)
        s_d = jnp.where(mask, s_d, NEG_INF)
        s_ref[diag:kv_len, :] = s_d
        m_d = jnp.max(s_d, axis=0, keepdims=True)
        out["m"] = m_d if "m" not in out else jnp.maximum(out["m"], m_d)

    def interior(r0, r1):
        s_o = lax.dot_general(k_rows(r0, r1), q, _NT, preferred_element_type=F32)
        s_ref[r0:r1, :] = s_o
        m_o = jnp.max(s_o, axis=0, keepdims=True)
        out["m"] = m_o if "m" not in out else jnp.maximum(out["m"], m_o)

    items = []
    for r0 in range(0, diag, SCORE_ROWS):
        items.append(functools.partial(interior, r0, min(r0 + SCORE_ROWS, diag)))
    return items + [boundary]


def _exp_items(kv_len, s_ref, p_ref, state):
    def block(r0):
        p = jnp.exp2(s_ref[r0:r0 + KV_TILE, :] - state["m"])
        part = jnp.sum(p, axis=0, keepdims=True)
        state["l"] = part if "l" not in state else state["l"] + part
        p_ref[r0:r0 + KV_TILE, :] = p.astype(BF16)

    return [functools.partial(block, r0) for r0 in range(0, kv_len, KV_TILE)]


def _emit_interleaved(item_lists):
    item_lists = [items for items in item_lists if items]
    done = [0] * len(item_lists)
    for _ in range(sum(len(items) for items in item_lists)):
        i = min((i for i in range(len(item_lists)) if done[i] < len(item_lists[i])),
                key=lambda i: (done[i] + 0.5) / len(item_lists[i]))
        item_lists[i][done[i]]()
        done[i] += 1


def _attention_pipeline(n_q, n_streams, q_of, k_rows_of, vt_cols_of, s_ref, p_ref, finalize):
    mask = _chunk_mask()
    state = {}

    def slot(qi, st):
        return (qi % 2) * n_streams + st

    def pv_item(qi, st):
        kv_len = (qi + 1) * Q_TILE
        state[qi, st]["acc"] = jnp.dot(vt_cols_of(st)(0, kv_len),
                                       p_ref[slot(qi, st), 0:kv_len, :],
                                       preferred_element_type=F32)

    order = list(range(n_q))
    for t in range(n_q + 2):
        scores, exps, pvs = [], [], []
        for st in range(n_streams):
            if t < n_q:
                qi = order[t]
                state[qi, st] = {}
                scores.append(_score_items(q_of(qi, st), k_rows_of(st), (qi + 1) * Q_TILE,
                                           s_ref.at[slot(qi, st)], mask, state[qi, st]))
            if 0 <= t - 1 < n_q:
                qi = order[t - 1]
                exps.append(_exp_items((qi + 1) * Q_TILE, s_ref.at[slot(qi, st)],
                                       p_ref.at[slot(qi, st)], state[qi, st]))
            if 0 <= t - 2 < n_q:
                pvs.append([functools.partial(pv_item, order[t - 2], st)])
        _emit_interleaved(scores + exps + pvs)
        if 0 <= t - 2 < n_q:
            qi = order[t - 2]
            done = [state.pop((qi, st)) for st in range(n_streams)]
            finalize(qi, [(d["l"], d["acc"]) for d in done])


def _store_transposed_values(v_ref, vt_ref):
    for r0 in range(0, v_ref.shape[1], KV_TILE):
        blk = v_ref[0, r0:r0 + KV_TILE, :].astype(F32)
        vt_ref[:, r0:r0 + KV_TILE] = blk.T.astype(BF16)


def _a_attn_kernel(lambda_init, q_ref, k_ref, v_ref, g_ref, lam_ref, subg_ref, o_ref,
                   vt_ref, s_ref, p_ref):
    s = q_ref.shape[1]
    first_map = lax.broadcasted_iota(jnp.int32, (Q_TILE, LANES), 1) < A_HEAD_DIM
    dv = 2 * A_HEAD_DIM
    _store_transposed_values(v_ref, vt_ref)

    def map_query(qi, st):
        hh, mp = divmod(st, 2)
        q = q_ref[0, qi * Q_TILE:(qi + 1) * Q_TILE, hh * LANES:(hh + 1) * LANES].astype(F32)
        return jnp.where(first_map == (mp == 0), q, 0.0).astype(BF16)

    lam = (jnp.exp(jnp.sum(lam_ref[0:1, :] * lam_ref[1:2, :], axis=1, keepdims=True))
           - jnp.exp(jnp.sum(lam_ref[2:3, :] * lam_ref[3:4, :], axis=1, keepdims=True))
           + lambda_init)
    out_gain = subg_ref[0:1, :] * (1.0 - lambda_init)

    def finalize(qi, outs):
        rows = slice(qi * Q_TILE, (qi + 1) * Q_TILE)
        for hh in range(A_HEADS_PER_STEP):
            cols = slice(hh * LANES, (hh + 1) * LANES)
            (l1, acc1), (l2, acc2) = outs[2 * hh:2 * hh + 2]
            o_t = acc1 * (1.0 / l1) - acc2 * (lam / l2)
            ms = jnp.mean(o_t * o_t, axis=0, keepdims=True)
            o_t = o_t * lax.rsqrt(ms + SUBLN_EPS)
            o = o_t.T * out_gain * g_ref[0, rows, cols].astype(F32)
            o_ref[0, rows, cols] = o.astype(BF16)

    _attention_pipeline(
        s // Q_TILE, 2 * A_HEADS_PER_STEP, map_query,
        lambda st: (lambda r0, r1: k_ref[0, r0:r1, (st // 2) * LANES:(st // 2 + 1) * LANES]),
        lambda st: (lambda c0, c1: vt_ref[(st // 2) * dv:(st // 2 + 1) * dv, c0:c1]),
        s_ref, p_ref, finalize)


def _a_attention(q, k, v, g, lam_rows, subg_rows, lambda_init):
    b, s, _ = q.shape
    n_streams = 2 * A_HEADS_PER_STEP
    spec = pl.BlockSpec((1, s, A_HEADS_PER_STEP * LANES), lambda bi, h: (bi, 0, h))
    small = pl.BlockSpec((8, LANES), lambda bi, h: (0, 0))
    return pl.pallas_call(
        functools.partial(_a_attn_kernel, lambda_init),
        grid=(b, A_HEADS // A_HEADS_PER_STEP),
        in_specs=[spec, spec, spec, spec, small, small],
        out_specs=spec,
        out_shape=jax.ShapeDtypeStruct((b, s, A_WIDTH), BF16),
        scratch_shapes=[pltpu.VMEM((A_HEADS_PER_STEP * LANES, s), BF16),
                        pltpu.VMEM((2 * n_streams, s, Q_TILE), F32),
                        pltpu.VMEM((2 * n_streams, s, Q_TILE), BF16)],
        compiler_params=pltpu.CompilerParams(vmem_limit_bytes=VMEM_LIMIT),
        name="a_attention",
    )(q, k, v, g, lam_rows, subg_rows)


def _deepnorm(y, x, mod_ref, lng_ref, lnb_ref):
    gate = mod_ref[0, 2:3, :]
    z = DEEPNORM_ALPHA * x + gate * y
    mu = jnp.mean(z, axis=1, keepdims=True)
    zc = z - mu
    var = jnp.mean(zc * zc, axis=1, keepdims=True)
    return zc * lax.rsqrt(var + LN_EPS) * lng_ref[...] + lnb_ref[...]


def _outproj_ln_kernel(o_ref, w_ref, x_ref, mod_ref, lng_ref, lnb_ref, y_ref):
    y = jnp.dot(o_ref[0], w_ref[...], preferred_element_type=F32)
    y_ref[0] = _deepnorm(y, x_ref[0], mod_ref, lng_ref, lnb_ref)


def _outproj_ln(o, w_bf16, x, mod, ln_g, ln_b):
    b, s, d = x.shape
    width = o.shape[2]
    row = pl.BlockSpec((1, d), lambda bi, i: (0, 0))
    return pl.pallas_call(
        _outproj_ln_kernel,
        grid=(b, s // OUT_ROW_TILE),
        in_specs=[pl.BlockSpec((1, OUT_ROW_TILE, width), lambda bi, i: (bi, i, 0)),
                  pl.BlockSpec((width, d), lambda bi, i: (0, 0)),
                  pl.BlockSpec((1, OUT_ROW_TILE, d), lambda bi, i: (bi, i, 0)),
                  pl.BlockSpec((1, 3, d), lambda bi, i: (bi, 0, 0)),
                  row, row],
        out_specs=pl.BlockSpec((1, OUT_ROW_TILE, d), lambda bi, i: (bi, i, 0)),
        out_shape=jax.ShapeDtypeStruct((b, s, d), F32),
        compiler_params=pltpu.CompilerParams(vmem_limit_bytes=VMEM_LIMIT),
        name="outproj_ln",
    )(o, w_bf16, x, mod, ln_g.reshape(1, d), ln_b.reshape(1, d))


B_HEAD_PAD = LANES
B_QK_PAD = B_HEADS * B_HEAD_PAD
B_COL_KV = B_Q_RANK
B_COL_ROPE = B_Q_RANK + B_KV_RANK
B_COL_GATE = B_COL_ROPE + LANES
B_IN_PAD = B_COL_GATE + B_WIDTH
B_COL_CHUNK = 512


def _rms(x, g_row, eps):
    ms = jnp.mean(x * x, axis=1, keepdims=True)
    return x * lax.rsqrt(ms + eps) * g_row


def _b_inproj_kernel(o_ref, wo_ref, x0_ref, mod0_ref, lng_ref, lnb_ref, mod_ref, pos_ref, freq_ref,
                     consts_ref, w1_ref, qg_ref, wuq_ref, kvg_ref, wk_ref, wv_ref,
                     x_ref, q_ref, k_ref, v_ref, g_ref):
    i = pl.program_id(1)
    shift = mod_ref[0, 0:1, :]
    scale = mod_ref[0, 1:2, :]
    y = jnp.dot(o_ref[0], wo_ref[...], preferred_element_type=F32)
    x = _deepnorm(y, x0_ref[0], mod0_ref, lng_ref, lnb_ref)
    x_ref[0] = x
    u = (x * (1.0 + scale) + shift).astype(BF16)
    cos_t, sin_up, sin_dn = _rope_tables(pos_ref, i * (ROW_TILE // LANES), ROW_TILE // LANES,
                                         freq_ref, consts_ref)
    half = B_ROPE // 2
    qk_scale = (B_NOPE + B_ROPE) ** -0.5 * LOG2E

    lat = jnp.dot(u, w1_ref[:, 0:B_COL_GATE], preferred_element_type=F32)
    qn = _rms(lat[:, 0:B_Q_RANK], qg_ref[...], RMS_EPS).astype(BF16)
    kvn = _rms(lat[:, B_COL_KV:B_COL_ROPE], kvg_ref[...], RMS_EPS).astype(BF16)
    k_rope = _rope_apply(lat[:, B_COL_ROPE:B_COL_GATE], cos_t, sin_up, sin_dn, half)

    cos_q, up_q, dn_q = cos_t * qk_scale, sin_up * qk_scale, sin_dn * qk_scale
    for c0 in range(0, B_QK_PAD, B_COL_CHUNK):
        qc = jnp.dot(qn, wuq_ref[:, c0:c0 + B_COL_CHUNK], preferred_element_type=F32)
        kc = jnp.dot(kvn, wk_ref[:, c0:c0 + B_COL_CHUNK], preferred_element_type=F32)
        for g0 in range(0, B_COL_CHUNK, B_HEAD_PAD):
            cols = slice(c0 + g0, c0 + g0 + B_HEAD_PAD)
            q_ref[0, :, cols] = _rope_apply(qc[:, g0:g0 + B_HEAD_PAD], cos_q, up_q, dn_q,
                                            half).astype(BF16)
            k_ref[0, :, cols] = (kc[:, g0:g0 + B_HEAD_PAD] + k_rope).astype(BF16)
    v_ref[0] = jnp.dot(kvn, wv_ref[...], preferred_element_type=F32).astype(BF16)
    gate = jnp.dot(u, w1_ref[:, B_COL_GATE:B_IN_PAD], preferred_element_type=F32)
    g_ref[0] = _silu(gate).astype(BF16)


def _b_inproj(o, wo, x0, mod0, ln_g, ln_b, mod, pos3, rope, w1, qg, wuq, kvg, wk, wv):
    b, s, d = x0.shape
    freq, consts = rope

    def full(a):
        return pl.BlockSpec(a.shape, lambda bi, i: (0,) * a.ndim,
                            pipeline_mode=pl.Buffered(1))

    def rows(width):
        return pl.BlockSpec((1, ROW_TILE, width), lambda bi, i: (bi, i, 0))

    mod_spec = pl.BlockSpec((1, 3, d), lambda bi, i: (bi, 0, 0))
    ln_g, ln_b = ln_g.reshape(1, d), ln_b.reshape(1, d)
    return pl.pallas_call(
        _b_inproj_kernel,
        grid=(b, s // ROW_TILE),
        in_specs=[rows(o.shape[2]), full(wo), rows(d), mod_spec, full(ln_g), full(ln_b),
                  mod_spec,
                  pl.BlockSpec((1, s // LANES, LANES), lambda bi, i: (bi, 0, 0)),
                  full(freq), full(consts), full(w1), full(qg), full(wuq), full(kvg), full(wk),
                  full(wv)],
        out_specs=[rows(d), rows(B_QK_PAD), rows(B_QK_PAD), rows(B_WIDTH), rows(B_WIDTH)],
        out_shape=[jax.ShapeDtypeStruct((b, s, d), F32),
                   jax.ShapeDtypeStruct((b, s, B_QK_PAD), BF16),
                   jax.ShapeDtypeStruct((b, s, B_QK_PAD), BF16),
                   jax.ShapeDtypeStruct((b, s, B_WIDTH), BF16),
                   jax.ShapeDtypeStruct((b, s, B_WIDTH), BF16)],
        compiler_params=pltpu.CompilerParams(vmem_limit_bytes=VMEM_LIMIT),
        name="b_inproj",
    )(o, wo, x0, mod0, ln_g, ln_b, mod, pos3, freq, consts, w1, qg, wuq, kvg, wk, wv)


def _b_attn_kernel(q_ref, k_ref, v_ref, g_ref, o_ref, vt_ref, s_ref, p_ref):
    s = q_ref.shape[1]
    _store_transposed_values(v_ref, vt_ref)

    def finalize(qi, outs):
        rows = slice(qi * Q_TILE, (qi + 1) * Q_TILE)
        o_t = jnp.concatenate([acc * (1.0 / l) for l, acc in outs], axis=0)
        o_ref[0, rows, :] = (o_t.T * g_ref[0, rows, :].astype(F32)).astype(BF16)

    _attention_pipeline(
        s // Q_TILE, B_HEADS_PER_STEP,
        lambda qi, st: q_ref[0, qi * Q_TILE:(qi + 1) * Q_TILE,
                             st * B_HEAD_PAD:(st + 1) * B_HEAD_PAD],
        lambda st: (lambda r0, r1: k_ref[0, r0:r1, st * B_HEAD_PAD:(st + 1) * B_HEAD_PAD]),
        lambda st: (lambda c0, c1: vt_ref[st * B_VDIM:(st + 1) * B_VDIM, c0:c1]),
        s_ref, p_ref, finalize)


def _b_attention(q, k, v, g):
    b, s, _ = q.shape
    qk_spec = pl.BlockSpec((1, s, B_HEADS_PER_STEP * B_HEAD_PAD), lambda bi, h: (bi, 0, h))
    spec = pl.BlockSpec((1, s, B_HEADS_PER_STEP * B_VDIM), lambda bi, h: (bi, 0, h))
    return pl.pallas_call(
        _b_attn_kernel,
        grid=(b, B_HEADS // B_HEADS_PER_STEP),
        in_specs=[qk_spec, qk_spec, spec, spec],
        out_specs=spec,
        out_shape=jax.ShapeDtypeStruct((b, s, B_WIDTH), BF16),
        scratch_shapes=[pltpu.VMEM((B_HEADS_PER_STEP * B_VDIM, s), BF16),
                        pltpu.VMEM((2 * B_HEADS_PER_STEP, s, Q_TILE), F32),
                        pltpu.VMEM((2 * B_HEADS_PER_STEP, s, Q_TILE), BF16)],
        compiler_params=pltpu.CompilerParams(vmem_limit_bytes=VMEM_LIMIT),
        name="b_attention",
    )(q, k, v, g)


def _pad_rows8(rows):
    out = jnp.zeros((8, LANES), F32)
    for r, vec in enumerate(rows):
        out = out.at[r, :vec.shape[0]].set(vec.astype(F32))
    return out


def _b_layouts(w_in, w_uq, w_ukv):
    d = w_in.shape[0]
    rope_cols = jnp.zeros((d, LANES), F32).at[:, B_NOPE:B_NOPE + B_ROPE].set(
        w_in[:, B_COL_ROPE:B_COL_ROPE + B_ROPE])
    w1 = jnp.concatenate([w_in[:, :B_COL_ROPE], rope_cols, w_in[:, B_COL_ROPE + B_ROPE:]], axis=1)
    uq = w_uq.reshape(B_Q_RANK, B_HEADS, B_NOPE + B_ROPE)
    uq = jnp.pad(uq, ((0, 0), (0, 0), (0, B_HEAD_PAD - B_NOPE - B_ROPE)))
    ukv = w_ukv.reshape(B_KV_RANK, B_HEADS, B_NOPE + B_VDIM)
    wk = jnp.pad(ukv[:, :, :B_NOPE], ((0, 0), (0, 0), (0, B_HEAD_PAD - B_NOPE)))
    wv = ukv[:, :, B_NOPE:]
    return (w1.astype(BF16), uq.reshape(B_Q_RANK, B_QK_PAD).astype(BF16),
            wk.reshape(B_KV_RANK, B_QK_PAD).astype(BF16),
            wv.reshape(B_KV_RANK, B_WIDTH).astype(BF16))


def kernel(x, c, positions, ada_w, ada_b, ln_g, ln_b, a_w_in, a_lambda_q1, a_lambda_k1,
           a_lambda_q2, a_lambda_k2, a_subln_g, a_w_out, b_w_in, b_q_norm_g, b_w_uq,
           b_kv_norm_g, b_w_ukv, b_w_out):
    b, s, d = x.shape
    assert d == D_MODEL and s % ROW_TILE == 0 and s % Q_TILE == 0 and Q_TILE == KV_TILE
    mod = _modulation(c, ada_w, ada_b).reshape(DEPTH, b, 3, d)
    pos3 = positions.reshape(b, s // LANES, LANES)

    lambda_init = 0.8 - 0.6 * math.exp(-0.3 * 0)
    q, k, v, g = _a_inproj(x, mod[0], pos3, _rope_consts(A_ROT, A_HEAD_DIM, 0),
                           a_w_in[0].astype(BF16))
    lam_rows = _pad_rows8([a_lambda_q1[0], a_lambda_k1[0], a_lambda_q2[0], a_lambda_k2[0]])
    o = _a_attention(q, k, v, g, lam_rows, _pad_rows8([a_subln_g[0]]), lambda_init)

    w1, wuq, wk, wv = _b_layouts(b_w_in[0], b_w_uq[0], b_w_ukv[0])
    x, q, k, v, g = _b_inproj(o, a_w_out[0].astype(BF16), x, mod[0], ln_g[0], ln_b[0], mod[1],
                              pos3, _rope_consts(B_ROPE, B_HEAD_PAD, B_NOPE), w1,
                              b_q_norm_g[0].reshape(1, B_Q_RANK), wuq,
                              b_kv_norm_g[0].reshape(1, B_KV_RANK), wk, wv)
    o = _b_attention(q, k, v, g)
    return _outproj_ln(o, b_w_out[0].astype(BF16), x, mod[1], ln_g[1], ln_b[1])
```

```python
import functools
import math

import jax
import jax.numpy as jnp
import numpy as np
from jax import lax
from jax.experimental import pallas as pl
from jax.experimental.pallas import tpu as pltpu

F32 = jnp.float32
BF16 = jnp.bfloat16

D_MODEL = 1024
CHUNK = 64
ROPE_THETA = 500000.0

A_HEADS = 8
A_HEAD_DIM = 64
A_ROT = A_HEAD_DIM // 4
A_WIDTH = A_HEADS * 2 * A_HEAD_DIM

B_HEADS = 16
B_NOPE = 64
B_ROPE = 32
B_VDIM = 64
B_Q_RANK = 512
B_KV_RANK = 256
B_WIDTH = B_HEADS * B_VDIM

DEPTH = 2
DEEPNORM_ALPHA = (2.0 * DEPTH) ** 0.25
LN_EPS = 1e-5
RMS_EPS = 1e-6
SUBLN_EPS = 1e-5
NEG_INF = -1e30

LANES = 128
FREQ_ROWS = 16
ROW_TILE = 512
OUT_ROW_TILE = 1024
Q_TILE = 256
KV_TILE = 256
SCORE_ROWS = 512
A_HEADS_PER_STEP = 2
B_HEADS_PER_STEP = 4
LOG2E = math.log2(math.e)
V7X_VMEM_BYTES = 64 * 1024 * 1024
VMEM_LIMIT = V7X_VMEM_BYTES // 4 * 3

_NT = (((1,), (1,)), ((), ()))


def _silu(x):
    return x * (1.0 / (1.0 + jnp.exp(-x)))


def _mod_kernel(c_ref, w_ref, b_ref, o_ref):
    ca = _silu(c_ref[...]).astype(BF16)
    o_ref[0] = jnp.dot(ca, w_ref[0].astype(BF16), preferred_element_type=F32) + b_ref[0]


def _modulation(c, ada_w, ada_b):
    depth, d, n3 = ada_w.shape
    b = c.shape[0]
    tn = 1024
    return pl.pallas_call(
        _mod_kernel,
        grid=(depth, n3 // tn),
        in_specs=[pl.BlockSpec((b, d), lambda i, j: (0, 0)),
                  pl.BlockSpec((1, d, tn), lambda i, j: (i, 0, j)),
                  pl.BlockSpec((1, 1, tn), lambda i, j: (i, 0, j))],
        out_specs=pl.BlockSpec((1, b, tn), lambda i, j: (i, 0, j)),
        out_shape=jax.ShapeDtypeStruct((depth, b, n3), F32),
        compiler_params=pltpu.CompilerParams(vmem_limit_bytes=VMEM_LIMIT),
        name="modulation",
    )(c, ada_w, ada_b.reshape(depth, 1, n3))


def _rope_tables(pos_ref, first_chunk, n_chunks, freq_ref, consts_ref):
    freq = freq_ref[...]
    cs, sn = [], []
    for r in range(n_chunks):
        pos = pos_ref[0, pl.ds(first_chunk + r, 1), :].astype(F32)
        ang = freq * pos
        reps = LANES // FREQ_ROWS
        cs.append(jnp.concatenate([jnp.cos(ang)] * reps, axis=0).T)
        sn.append(jnp.concatenate([jnp.sin(ang)] * reps, axis=0).T)
    cos_t = jnp.where(consts_ref[0:1, :] > 0.0, jnp.concatenate(cs, axis=0), 1.0)
    sin_t = jnp.concatenate(sn, axis=0)
    return cos_t, sin_t * consts_ref[1:2, :], sin_t * consts_ref[2:3, :]


def _rope_apply(a, cos_t, sin_up, sin_dn, half):
    return (a * cos_t + pltpu.roll(a, LANES - half, 1) * sin_up
            + pltpu.roll(a, half, 1) * sin_dn)


def _rope_consts(rot_dim, group, offset):
    half = rot_dim // 2
    assert FREQ_ROWS % half == 0 and offset % half == 0 and group % half == 0
    inv_freq = ROPE_THETA ** (-jnp.arange(0, rot_dim, 2, dtype=F32) / rot_dim)
    freq_rows = jnp.broadcast_to(jnp.tile(inv_freq, FREQ_ROWS // half)[:, None],
                                 (FREQ_ROWS, LANES))
    d = np.arange(LANES) % group - offset
    rotated = (d >= 0) & (d < rot_dim)
    first = rotated & (d < half)
    second = rotated & (d >= half)
    consts = np.zeros((8, LANES), np.float32)
    consts[0], consts[1], consts[2] = rotated, -first.astype(np.float32), second
    return freq_rows, jnp.asarray(consts)


def _a_inproj_kernel(x_ref, mod_ref, pos_ref, freq_ref, consts_ref, w_ref, q_ref, k_ref, v_ref,
                     g_ref):
    i = pl.program_id(1)
    shift = mod_ref[0, 0:1, :]
    scale = mod_ref[0, 1:2, :]
    u = (x_ref[0] * (1.0 + scale) + shift).astype(BF16)
    cos_t, sin_up, sin_dn = _rope_tables(pos_ref, i * (ROW_TILE // LANES), ROW_TILE // LANES,
                                         freq_ref, consts_ref)
    half = A_ROT // 2
    qk_scale = A_HEAD_DIM ** -0.5 * LOG2E

    def rope_store(col0, o_ref, mult):
        acc = jnp.dot(u, w_ref[:, col0:col0 + A_WIDTH], preferred_element_type=F32)
        for g in range(A_WIDTH // LANES):
            a = acc[:, g * LANES:(g + 1) * LANES]
            r = _rope_apply(a, cos_t, sin_up, sin_dn, half)
            if mult != 1.0:
                r = r * mult
            o_ref[0, :, g * LANES:(g + 1) * LANES] = r.astype(BF16)

    rope_store(0, q_ref, qk_scale)
    rope_store(A_WIDTH, k_ref, 1.0)
    v_ref[0] = jnp.dot(u, w_ref[:, 2 * A_WIDTH:3 * A_WIDTH],
                       preferred_element_type=F32).astype(BF16)
    gate = jnp.dot(u, w_ref[:, 3 * A_WIDTH:4 * A_WIDTH], preferred_element_type=F32)
    g_ref[0] = _silu(gate).astype(BF16)


def _a_inproj(x, mod, pos3, rope, w_bf16):
    b, s, d = x.shape
    freq, consts = rope
    n = w_bf16.shape[1]
    out = jax.ShapeDtypeStruct((b, s, A_WIDTH), BF16)
    ospec = pl.BlockSpec((1, ROW_TILE, A_WIDTH), lambda bi, i: (bi, i, 0))
    return pl.pallas_call(
        _a_inproj_kernel,
        grid=(b, s // ROW_TILE),
        in_specs=[pl.BlockSpec((1, ROW_TILE, d), lambda bi, i: (bi, i, 0)),
                  pl.BlockSpec((1, 3, d), lambda bi, i: (bi, 0, 0)),
                  pl.BlockSpec((1, s // LANES, LANES), lambda bi, i: (bi, 0, 0)),
                  pl.BlockSpec((FREQ_ROWS, LANES), lambda bi, i: (0, 0)),
                  pl.BlockSpec((8, LANES), lambda bi, i: (0, 0)),
                  pl.BlockSpec((d, n), lambda bi, i: (0, 0))],
        out_specs=[ospec, ospec, ospec, ospec],
        out_shape=[out, out, out, out],
        compiler_params=pltpu.CompilerParams(vmem_limit_bytes=VMEM_LIMIT),
        name="a_inproj",
    )(x, mod, pos3, freq, consts, w_bf16)


def _chunk_mask():
    krow = lax.broadcasted_iota(jnp.int32, (KV_TILE, Q_TILE), 0) // CHUNK
    qcol = lax.broadcasted_iota(jnp.int32, (KV_TILE, Q_TILE), 1) // CHUNK
    return krow <= qcol


def _score_items(q, k_rows, kv_len, s_ref, mask, out):
    diag = kv_len - KV_TILE

    def boundary():
        s_d = lax.dot_general(k_rows(diag, kv_len), q, _NT, preferred_element_type=F32)
        s_d = jnp.where(mask, s_d, NEG_INF)
        s_ref[diag:kv_len, :] = s_d
        m_d = jnp.max(s_d, axis=0, keepdims=True)
        out["m"] = m_d if "m" not in out else jnp.maximum(out["m"], m_d)

    def interior(r0, r1):
        s_o = lax.dot_general(k_rows(r0, r1), q, _NT, preferred_element_type=F32)
        s_ref[r0:r1, :] = s_o
        m_o = jnp.max(s_o, axis=0, keepdims=True)
        out["m"] = m_o if "m" not in out else jnp.maximum(out["m"], m_o)

    items = []
    for r0 in range(0, diag, SCORE_ROWS):
        items.append(functools.partial(interior, r0, min(r0 + SCORE_ROWS, diag)))
    return items + [boundary]


def _exp_items(kv_len, s_ref, p_ref, state):
    def block(r0):
        if "m_tile" not in state:
            state["m_tile"] = jnp.broadcast_to(state["m"], (KV_TILE, Q_TILE))
        p = jnp.exp2(s_ref[r0:r0 + KV_TILE, :] - state["m_tile"])
        part = jnp.sum(p, axis=0, keepdims=True)
        state["l"] = part if "l" not in state else state["l"] + part
        p_ref[r0:r0 + KV_TILE, :] = p.astype(BF16)

    return [functools.partial(block, r0) for r0 in range(0, kv_len, KV_TILE)]


def _emit_interleaved(item_lists):
    item_lists = [items for items in item_lists if items]
    done = [0] * len(item_lists)
    for _ in range(sum(len(items) for items in item_lists)):
        i = min((i for i in range(len(item_lists)) if done[i] < len(item_lists[i])),
                key=lambda i: (done[i] + 0.5) / len(item_lists[i]))
        item_lists[i][done[i]]()
        done[i] += 1


def _attention_pipeline(n_q, n_streams, q_of, k_rows_of, vt_cols_of, s_ref, p_ref, finalize):
    mask = _chunk_mask()
    state = {}

    def slot(qi, st):
        return (qi % 2) * n_streams + st

    def pv_item(qi, st):
        kv_len = (qi + 1) * Q_TILE
        state[qi, st]["acc"] = jnp.dot(vt_cols_of(st)(0, kv_len),
                                       p_ref[slot(qi, st), 0:kv_len, :],
                                       preferred_element_type=F32)

    order = list(range(n_q))
    for t in range(n_q + 2):
        scores, exps, pvs = [], [], []
        for st in range(n_streams):
            if t < n_q:
                qi = order[t]
                state[qi, st] = {}
                scores.append(_score_items(q_of(qi, st), k_rows_of(st), (qi + 1) * Q_TILE,
                                           s_ref.at[slot(qi, st)], mask, state[qi, st]))
            if 0 <= t - 1 < n_q:
                qi = order[t - 1]
                exps.append(_exp_items((qi + 1) * Q_TILE, s_ref.at[slot(qi, st)],
                                       p_ref.at[slot(qi, st)], state[qi, st]))
            if 0 <= t - 2 < n_q:
                pvs.append([functools.partial(pv_item, order[t - 2], st)])
        _emit_interleaved(scores + exps + pvs)
        if 0 <= t - 2 < n_q:
            qi = order[t - 2]
            done = [state.pop((qi, st)) for st in range(n_streams)]
            finalize(qi, [(d["l"], d["acc"]) for d in done])


def _store_transposed_values(v_ref, vt_ref):
    for r0 in range(0, v_ref.shape[1], KV_TILE):
        blk = v_ref[0, r0:r0 + KV_TILE, :].astype(F32)
        vt_ref[:, r0:r0 + KV_TILE] = blk.T.astype(BF16)


def _a_attn_kernel(lambda_init, q_ref, k_ref, v_ref, g_ref, lam_ref, subg_ref, o_ref,
                   vt_ref, s_ref, p_ref):
    s = q_ref.shape[1]
    first_map = lax.broadcasted_iota(jnp.int32, (Q_TILE, LANES), 1) < A_HEAD_DIM
    dv = 2 * A_HEAD_DIM
    _store_transposed_values(v_ref, vt_ref)

    def map_query(qi, st):
        hh, mp = divmod(st, 2)
        q = q_ref[0, qi * Q_TILE:(qi + 1) * Q_TILE, hh * LANES:(hh + 1) * LANES].astype(F32)
        return jnp.where(first_map == (mp == 0), q, 0.0).astype(BF16)

    lam = (jnp.exp(jnp.sum(lam_ref[0:1, :] * lam_ref[1:2, :], axis=1, keepdims=True))
           - jnp.exp(jnp.sum(lam_ref[2:3, :] * lam_ref[3:4, :], axis=1, keepdims=True))
           + lambda_init)
    out_gain = subg_ref[0:1, :] * (1.0 - lambda_init)

    def finalize(qi, outs):
        rows = slice(qi * Q_TILE, (qi + 1) * Q_TILE)
        for hh in range(A_HEADS_PER_STEP):
            cols = slice(hh * LANES, (hh + 1) * LANES)
            (l1, acc1), (l2, acc2) = outs[2 * hh:2 * hh + 2]
            o_t = acc1 * (1.0 / l1) - acc2 * (lam / l2)
            ms = jnp.mean(o_t * o_t, axis=0, keepdims=True)
            o_t = o_t * lax.rsqrt(ms + SUBLN_EPS)
            o = o_t.T * out_gain * g_ref[0, rows, cols].astype(F32)
            o_ref[0, rows, cols] = o.astype(BF16)

    _attention_pipeline(
        s // Q_TILE, 2 * A_HEADS_PER_STEP, map_query,
        lambda st: (lambda r0, r1: k_ref[0, r0:r1, (st // 2) * LANES:(st // 2 + 1) * LANES]),
        lambda st: (lambda c0, c1: vt_ref[(st // 2) * dv:(st // 2 + 1) * dv, c0:c1]),
        s_ref, p_ref, finalize)


def _a_attention(q, k, v, g, lam_rows, subg_rows, lambda_init):
    b, s, _ = q.shape
    n_streams = 2 * A_HEADS_PER_STEP
    spec = pl.BlockSpec((1, s, A_HEADS_PER_STEP * LANES), lambda bi, h: (bi, 0, h))
    small = pl.BlockSpec((8, LANES), lambda bi, h: (0, 0))
    return pl.pallas_call(
        functools.partial(_a_attn_kernel, lambda_init),
        grid=(b, A_HEADS // A_HEADS_PER_STEP),
        in_specs=[spec, spec, spec, spec, small, small],
        out_specs=spec,
        out_shape=jax.ShapeDtypeStruct((b, s, A_WIDTH), BF16),
        scratch_shapes=[pltpu.VMEM((A_HEADS_PER_STEP * LANES, s), BF16),
                        pltpu.VMEM((2 * n_streams, s, Q_TILE), F32),
                        pltpu.VMEM((2 * n_streams, s, Q_TILE), BF16)],
        compiler_params=pltpu.CompilerParams(vmem_limit_bytes=VMEM_LIMIT),
        name="a_attention",
    )(q, k, v, g, lam_rows, subg_rows)


def _deepnorm(y, x, mod_ref, lng_ref, lnb_ref):
    gate = mod_ref[0, 2:3, :]
    z = DEEPNORM_ALPHA * x + gate * y
    mu = jnp.mean(z, axis=1, keepdims=True)
    zc = z - mu
    var = jnp.mean(zc * zc, axis=1, keepdims=True)
    return zc * lax.rsqrt(var + LN_EPS) * lng_ref[...] + lnb_ref[...]


def _outproj_ln_kernel(o_ref, w_ref, x_ref, mod_ref, lng_ref, lnb_ref, y_ref):
    y = jnp.dot(o_ref[0], w_ref[...], preferred_element_type=F32)
    y_ref[0] = _deepnorm(y, x_ref[0], mod_ref, lng_ref, lnb_ref)


def _outproj_ln(o, w_bf16, x, mod, ln_g, ln_b):
    b, s, d = x.shape
    width = o.shape[2]
    row = pl.BlockSpec((1, d), lambda bi, i: (0, 0))
    return pl.pallas_call(
        _outproj_ln_kernel,
        grid=(b, s // OUT_ROW_TILE),
        in_specs=[pl.BlockSpec((1, OUT_ROW_TILE, width), lambda bi, i: (bi, i, 0)),
                  pl.BlockSpec((width, d), lambda bi, i: (0, 0)),
                  pl.BlockSpec((1, OUT_ROW_TILE, d), lambda bi, i: (bi, i, 0)),
                  pl.BlockSpec((1, 3, d), lambda bi, i: (bi, 0, 0)),
                  row, row],
        out_specs=pl.BlockSpec((1, OUT_ROW_TILE, d), lambda bi, i: (bi, i, 0)),
        out_shape=jax.ShapeDtypeStruct((b, s, d), F32),
        compiler_params=pltpu.CompilerParams(vmem_limit_bytes=VMEM_LIMIT),
        name="outproj_ln",
    )(o, w_bf16, x, mod, ln_g.reshape(1, d), ln_b.reshape(1, d))


B_HEAD_PAD = LANES
B_QK_PAD = B_HEADS * B_HEAD_PAD
B_COL_KV = B_Q_RANK
B_COL_ROPE = B_Q_RANK + B_KV_RANK
B_COL_GATE = B_COL_ROPE + LANES
B_IN_PAD = B_COL_GATE + B_WIDTH
B_COL_CHUNK = 512


def _rms(x, g_row, eps):
    ms = jnp.mean(x * x, axis=1, keepdims=True)
    return x * lax.rsqrt(ms + eps) * g_row


def _b_inproj_kernel(o_ref, wo_ref, x0_ref, mod0_ref, lng_ref, lnb_ref, mod_ref, pos_ref, freq_ref,
                     consts_ref, w1_ref, qg_ref, wuq_ref, kvg_ref, wk_ref, wv_ref,
                     x_ref, q_ref, k_ref, v_ref, g_ref):
    i = pl.program_id(1)
    shift = mod_ref[0, 0:1, :]
    scale = mod_ref[0, 1:2, :]
    y = jnp.dot(o_ref[0], wo_ref[...], preferred_element_type=F32)
    x = _deepnorm(y, x0_ref[0], mod0_ref, lng_ref, lnb_ref)
    x_ref[0] = x
    u = (x * (1.0 + scale) + shift).astype(BF16)
    cos_t, sin_up, sin_dn = _rope_tables(pos_ref, i * (ROW_TILE // LANES), ROW_TILE // LANES,
                                         freq_ref, consts_ref)
    half = B_ROPE // 2
    qk_scale = (B_NOPE + B_ROPE) ** -0.5 * LOG2E

    lat = jnp.dot(u, w1_ref[:, 0:B_COL_GATE], preferred_element_type=F32)
    qn = _rms(lat[:, 0:B_Q_RANK], qg_ref[...], RMS_EPS).astype(BF16)
    kvn = _rms(lat[:, B_COL_KV:B_COL_ROPE], kvg_ref[...], RMS_EPS).astype(BF16)
    k_rope = _rope_apply(lat[:, B_COL_ROPE:B_COL_GATE], cos_t, sin_up, sin_dn, half)

    cos_q, up_q, dn_q = cos_t * qk_scale, sin_up * qk_scale, sin_dn * qk_scale
    for c0 in range(0, B_QK_PAD, B_COL_CHUNK):
        qc = jnp.dot(qn, wuq_ref[:, c0:c0 + B_COL_CHUNK], preferred_element_type=F32)
        kc = jnp.dot(kvn, wk_ref[:, c0:c0 + B_COL_CHUNK], preferred_element_type=F32)
        for g0 in range(0, B_COL_CHUNK, B_HEAD_PAD):
            cols = slice(c0 + g0, c0 + g0 + B_HEAD_PAD)
            q_ref[0, :, cols] = _rope_apply(qc[:, g0:g0 + B_HEAD_PAD], cos_q, up_q, dn_q,
                                            half).astype(BF16)
            k_ref[0, :, cols] = (kc[:, g0:g0 + B_HEAD_PAD] + k_rope).astype(BF16)
    v_ref[0] = jnp.dot(kvn, wv_ref[...], preferred_element_type=F32).astype(BF16)
    gate = jnp.dot(u, w1_ref[:, B_COL_GATE:B_IN_PAD], preferred_element_type=F32)
    g_ref[0] = _silu(gate).astype(BF16)


def _b_inproj(o, wo, x0, mod0, ln_g, ln_b, mod, pos3, rope, w1, qg, wuq, kvg, wk, wv):
    b, s, d = x0.shape
    freq, consts = rope

    def full(a):
        return pl.BlockSpec(a.shape, lambda bi, i: (0,) * a.ndim,
                            pipeline_mode=pl.Buffered(1))

    def rows(width):
        return pl.BlockSpec((1, ROW_TILE, width), lambda bi, i: (bi, i, 0))

    mod_spec = pl.BlockSpec((1, 3, d), lambda bi, i: (bi, 0, 0))
    ln_g, ln_b = ln_g.reshape(1, d), ln_b.reshape(1, d)
    return pl.pallas_call(
        _b_inproj_kernel,
        grid=(b, s // ROW_TILE),
        in_specs=[rows(o.shape[2]), full(wo), rows(d), mod_spec, full(ln_g), full(ln_b),
                  mod_spec,
                  pl.BlockSpec((1, s // LANES, LANES), lambda bi, i: (bi, 0, 0)),
                  full(freq), full(consts), full(w1), full(qg), full(wuq), full(kvg), full(wk),
                  full(wv)],
        out_specs=[rows(d), rows(B_QK_PAD), rows(B_QK_PAD), rows(B_WIDTH), rows(B_WIDTH)],
        out_shape=[jax.ShapeDtypeStruct((b, s, d), F32),
                   jax.ShapeDtypeStruct((b, s, B_QK_PAD), BF16),
                   jax.ShapeDtypeStruct((b, s, B_QK_PAD), BF16),
                   jax.ShapeDtypeStruct((b, s, B_WIDTH), BF16),
                   jax.ShapeDtypeStruct((b, s, B_WIDTH), BF16)],
        compiler_params=pltpu.CompilerParams(vmem_limit_bytes=VMEM_LIMIT),
        name="b_inproj",
    )(o, wo, x0, mod0, ln_g, ln_b, mod, pos3, freq, consts, w1, qg, wuq, kvg, wk, wv)


def _b_attn_kernel(q_ref, k_ref, v_ref, g_ref, o_ref, vt_ref, s_ref, p_ref):
    s = q_ref.shape[1]
    _store_transposed_values(v_ref, vt_ref)

    def finalize(qi, outs):
        rows = slice(qi * Q_TILE, (qi + 1) * Q_TILE)
        o_t = jnp.concatenate([acc * (1.0 / l) for l, acc in outs], axis=0)
        o_ref[0, rows, :] = (o_t.T * g_ref[0, rows, :].astype(F32)).astype(BF16)

    _attention_pipeline(
        s // Q_TILE, B_HEADS_PER_STEP,
        lambda qi, st: q_ref[0, qi * Q_TILE:(qi + 1) * Q_TILE,
                             st * B_HEAD_PAD:(st + 1) * B_HEAD_PAD],
        lambda st: (lambda r0, r1: k_ref[0, r0:r1, st * B_HEAD_PAD:(st + 1) * B_HEAD_PAD]),
        lambda st: (lambda c0, c1: vt_ref[st * B_VDIM:(st + 1) * B_VDIM, c0:c1]),
        s_ref, p_ref, finalize)


def _b_attention(q, k, v, g):
    b, s, _ = q.shape
    qk_spec = pl.BlockSpec((1, s, B_HEADS_PER_STEP * B_HEAD_PAD), lambda bi, h: (bi, 0, h))
    spec = pl.BlockSpec((1, s, B_HEADS_PER_STEP * B_VDIM), lambda bi, h: (bi, 0, h))
    return pl.pallas_call(
        _b_attn_kernel,
        grid=(b, B_HEADS // B_HEADS_PER_STEP),
        in_specs=[qk_spec, qk_spec, spec, spec],
        out_specs=spec,
        out_shape=jax.ShapeDtypeStruct((b, s, B_WIDTH), BF16),
        scratch_shapes=[pltpu.VMEM((B_HEADS_PER_STEP * B_VDIM, s), BF16),
                        pltpu.VMEM((2 * B_HEADS_PER_STEP, s, Q_TILE), F32),
                        pltpu.VMEM((2 * B_HEADS_PER_STEP, s, Q_TILE), BF16)],
        compiler_params=pltpu.CompilerParams(vmem_limit_bytes=VMEM_LIMIT),
        name="b_attention",
    )(q, k, v, g)


def _pad_rows8(rows):
    out = jnp.zeros((8, LANES), F32)
    for r, vec in enumerate(rows):
        out = out.at[r, :vec.shape[0]].set(vec.astype(F32))
    return out


def _b_layouts(w_in, w_uq, w_ukv):
    d = w_in.shape[0]
    rope_cols = jnp.zeros((d, LANES), F32).at[:, B_NOPE:B_NOPE + B_ROPE].set(
        w_in[:, B_COL_ROPE:B_COL_ROPE + B_ROPE])
    w1 = jnp.concatenate([w_in[:, :B_COL_ROPE], rope_cols, w_in[:, B_COL_ROPE + B_ROPE:]], axis=1)
    uq = w_uq.reshape(B_Q_RANK, B_HEADS, B_NOPE + B_ROPE)
    uq = jnp.pad(uq, ((0, 0), (0, 0), (0, B_HEAD_PAD - B_NOPE - B_ROPE)))
    ukv = w_ukv.reshape(B_KV_RANK, B_HEADS, B_NOPE + B_VDIM)
    wk = jnp.pad(ukv[:, :, :B_NOPE], ((0, 0), (0, 0), (0, B_HEAD_PAD - B_NOPE)))
    wv = ukv[:, :, B_NOPE:]
    return (w1.astype(BF16), uq.reshape(B_Q_RANK, B_QK_PAD).astype(BF16),
            wk.reshape(B_KV_RANK, B_QK_PAD).astype(BF16),
            wv.reshape(B_KV_RANK, B_WIDTH).astype(BF16))


def kernel(x, c, positions, ada_w, ada_b, ln_g, ln_b, a_w_in, a_lambda_q1, a_lambda_k1,
           a_lambda_q2, a_lambda_k2, a_subln_g, a_w_out, b_w_in, b_q_norm_g, b_w_uq,
           b_kv_norm_g, b_w_ukv, b_w_out):
    b, s, d = x.shape
    assert d == D_MODEL and s % ROW_TILE == 0 and s % Q_TILE == 0 and Q_TILE == KV_TILE
    mod = _modulation(c, ada_w, ada_b).reshape(DEPTH, b, 3, d)
    pos3 = positions.reshape(b, s // LANES, LANES)

    lambda_init = 0.8 - 0.6 * math.exp(-0.3 * 0)
    q, k, v, g = _a_inproj(x, mod[0], pos3, _rope_consts(A_ROT, A_HEAD_DIM, 0),
                           a_w_in[0].astype(BF16))
    lam_rows = _pad_rows8([a_lambda_q1[0], a_lambda_k1[0], a_lambda_q2[0], a_lambda_k2[0]])
    o = _a_attention(q, k, v, g, lam_rows, _pad_rows8([a_subln_g[0]]), lambda_init)

    w1, wuq, wk, wv = _b_layouts(b_w_in[0], b_w_uq[0], b_w_ukv[0])
    x, q, k, v, g = _b_inproj(o, a_w_out[0].astype(BF16), x, mod[0], ln_g[0], ln_b[0], mod[1],
                              pos3, _rope_consts(B_ROPE, B_HEAD_PAD, B_NOPE), w1,
                              b_q_norm_g[0].reshape(1, B_Q_RANK), wuq,
                              b_kv_norm_g[0].reshape(1, B_KV_RANK), wk, wv)
    o = _b_attention(q, k, v, g)
    return _outproj_ln(o, b_w_out[0].astype(BF16), x, mod[1], ln_g[1], ln_b[1])
```
